```python
import jax, jax.numpy as jnp
from jax import lax
import numpy as np

D_MODEL = 1024
BATCH = 4
SEQ = 4096
DEPTH = 4
DEC_BATCH = 32
DEC_SEQ = 1
PAST_LEN = 8192
PAGE_SIZE = 128

N_MIXERS = 2
N_HEADS = 16
HEAD_DIM = D_MODEL // N_HEADS
D_CONV = D_MODEL
CONV_W = 3
D_FF = 4 * D_MODEL
Q_BLOCK = 128
N_ATTN_LAYERS = (DEPTH + N_MIXERS - 1) // N_MIXERS
N_CONV_LAYERS = DEPTH // N_MIXERS
RMS_EPS = 1e-6
SB_BIAS_INIT = -6.0

kernel_name = "stickbreak_shortconv_hybrid_step"


def rmsnorm(x, g):
    x32 = x.astype(jnp.float32)
    inv = lax.rsqrt(jnp.mean(x32 * x32, axis=-1, keepdims=True) + RMS_EPS)
    return (x32 * inv).astype(x.dtype) * g


def stick_breaking(q, k, v, bias, q_pos, k_pos):
    scale = HEAD_DIM ** -0.5
    z = jnp.einsum('bqhd,bkhd->bhqk', q.astype(jnp.float32), k.astype(jnp.float32)) * scale
    z = z + bias.astype(jnp.float32)[None, :, None, None]
    mask = k_pos[None, :] < q_pos[:, None]
    log_keep = jnp.where(mask, jax.nn.log_sigmoid(-z), 0.0)
    later = lax.cumsum(log_keep, axis=3, reverse=True) - log_keep
    w = jnp.where(mask, jnp.exp(jax.nn.log_sigmoid(z) + later), 0.0)
    o = jnp.einsum('bhqk,bkhd->bqhd', w, v.astype(jnp.float32))
    return o.astype(v.dtype)


def prompt_stick_breaking(q, k, v, bias):
    b, s = q.shape[0], q.shape[1]
    nb = s // Q_BLOCK
    pos = jnp.arange(s, dtype=jnp.int32)
    qb = jnp.moveaxis(q.reshape(b, nb, Q_BLOCK, N_HEADS, HEAD_DIM), 1, 0)
    pb = pos.reshape(nb, Q_BLOCK)
    ob = lax.map(lambda a: stick_breaking(a[0], k, v, bias, a[1], pos), (qb, pb))
    return jnp.moveaxis(ob, 0, 1).reshape(b, s, N_HEADS, HEAD_DIM)


def qkv_split(h, w_qkv):
    b, t = h.shape[0], h.shape[1]
    q, k, v = jnp.split(h @ w_qkv, 3, axis=-1)
    shp = (b, t, N_HEADS, HEAD_DIM)
    return q.reshape(shp), k.reshape(shp), v.reshape(shp)


def short_conv(u_full, w):
    t = u_full.shape[1] - (CONV_W - 1)
    y = sum(w[j] * u_full[:, j:j + t] for j in range(CONV_W))
    return y, u_full[:, -(CONV_W - 1):]


def gated_conv_mixer(h, conv_state, w_in, w_conv, w_out):
    gate_b, gate_c, xin = jnp.split(h @ w_in, 3, axis=-1)
    u = gate_c * xin
    y, new_state = short_conv(jnp.concatenate([conv_state, u], axis=1), w_conv)
    return (gate_b * y) @ w_out, new_state


def sq_relu_mlp(h, w_up, w_down):
    return jnp.square(jax.nn.relu(h @ w_up)) @ w_down


def setup_inputs(seed: int = 0) -> dict:
    key = jax.random.key(seed)
    ks = jax.random.split(key, 20)
    n_pages = PAST_LEN // PAGE_SIZE
    n_used = DEC_BATCH * n_pages
    n_phys = n_used + (n_used + 3) // 4
    f32 = jnp.float32
    nrm = lambda k, shp, s: jax.random.normal(k, shp, f32) * s
    page_table = jax.random.permutation(ks[5], n_phys)[:n_used].reshape(DEC_BATCH, n_pages).astype(jnp.int32)
    return {
        "x_prompt": nrm(ks[0], (BATCH, SEQ, D_MODEL), 1.0),
        "x_sample": nrm(ks[1], (DEC_BATCH, DEC_SEQ, D_MODEL), 1.0),
        "cache_k": nrm(ks[2], (N_ATTN_LAYERS, n_phys, PAGE_SIZE, N_HEADS, HEAD_DIM), 1.0),
        "cache_v": nrm(ks[3], (N_ATTN_LAYERS, n_phys, PAGE_SIZE, N_HEADS, HEAD_DIM), 1.0),
        "state_conv": nrm(ks[4], (N_CONV_LAYERS, DEC_BATCH, CONV_W - 1, D_CONV), 1.0),
        "page_table": page_table,
        "norm_mix_pre": 1.0 + nrm(ks[6], (DEPTH, D_MODEL), 0.05),
        "norm_mix_post": 1.0 + nrm(ks[7], (DEPTH, D_MODEL), 0.05),
        "norm_mlp_pre": 1.0 + nrm(ks[8], (DEPTH, D_MODEL), 0.05),
        "norm_mlp_post": 1.0 + nrm(ks[9], (DEPTH, D_MODEL), 0.05),
        "w_qkv": nrm(ks[10], (N_ATTN_LAYERS, D_MODEL, 3 * D_MODEL), D_MODEL ** -0.5),
        "sb_bias": SB_BIAS_INIT + nrm(ks[17], (N_ATTN_LAYERS, N_HEADS), 0.5),
        "w_attn_out": nrm(ks[11], (N_ATTN_LAYERS, D_MODEL, D_MODEL), D_MODEL ** -0.5),
        "w_conv_in": nrm(ks[12], (N_CONV_LAYERS, D_MODEL, 3 * D_CONV), D_MODEL ** -0.5),
        "conv_w": nrm(ks[13], (N_CONV_LAYERS, CONV_W, D_CONV), CONV_W ** -0.5),
        "w_conv_out": nrm(ks[14], (N_CONV_LAYERS, D_CONV, D_MODEL), D_CONV ** -0.5),
        "w_mlp_up": nrm(ks[15], (DEPTH, D_MODEL, D_FF), D_MODEL ** -0.5),
        "w_mlp_down": nrm(ks[16], (DEPTH, D_FF, D_MODEL), D_FF ** -0.5),
    }


def reference(x_prompt, x_sample, cache_k, cache_v, state_conv, page_table,
              norm_mix_pre, norm_mix_post, norm_mlp_pre, norm_mlp_post,
              w_qkv, sb_bias, w_attn_out, w_conv_in, conv_w, w_conv_out, w_mlp_up, w_mlp_down):
    xp, xs = x_prompt, x_sample
    db, ts = x_sample.shape[0], x_sample.shape[1]
    q_pos_s = PAST_LEN + jnp.arange(ts, dtype=jnp.int32)
    k_pos_s = jnp.arange(PAST_LEN + ts, dtype=jnp.int32)
    kp_new, vp_new, cp_new, ks_new, vs_new, cs_new = [], [], [], [], [], []
    for i in range(DEPTH):
        hp = rmsnorm(xp, norm_mix_pre[i])
        hs = rmsnorm(xs, norm_mix_pre[i])
        if i % N_MIXERS == 0:
            a = i // N_MIXERS
            qp, kp, vp = qkv_split(hp, w_qkv[a])
            op = prompt_stick_breaking(qp, kp, vp, sb_bias[a]).reshape(xp.shape[0], xp.shape[1], D_MODEL)
            op = op @ w_attn_out[a]
            qs, ks, vs = qkv_split(hs, w_qkv[a])
            past_k = cache_k[a][page_table].reshape(db, PAST_LEN, N_HEADS, HEAD_DIM)
            past_v = cache_v[a][page_table].reshape(db, PAST_LEN, N_HEADS, HEAD_DIM)
            k_all = jnp.concatenate([past_k.astype(ks.dtype), ks], axis=1)
            v_all = jnp.concatenate([past_v.astype(vs.dtype), vs], axis=1)
            os_ = stick_breaking(qs, k_all, v_all, sb_bias[a], q_pos_s, k_pos_s).reshape(db, ts, D_MODEL)
            os_ = os_ @ w_attn_out[a]
            kp_new.append(kp); vp_new.append(vp); ks_new.append(ks); vs_new.append(vs)
        else:
            c = i // N_MIXERS
            zero_state = jnp.zeros((xp.shape[0], CONV_W - 1, D_CONV), dtype=hp.dtype)
            op, st_p = gated_conv_mixer(hp, zero_state, w_conv_in[c], conv_w[c], w_conv_out[c])
            os_, st_s = gated_conv_mixer(hs, state_conv[c].astype(hs.dtype), w_conv_in[c], conv_w[c], w_conv_out[c])
            cp_new.append(st_p); cs_new.append(st_s)
        xp = xp + rmsnorm(op, norm_mix_post[i])
        xs = xs + rmsnorm(os_, norm_mix_post[i])
        mp = sq_relu_mlp(rmsnorm(xp, norm_mlp_pre[i]), w_mlp_up[i], w_mlp_down[i])
        ms = sq_relu_mlp(rmsnorm(xs, norm_mlp_pre[i]), w_mlp_up[i], w_mlp_down[i])
        xp = xp + rmsnorm(mp, norm_mlp_post[i])
        xs = xs + rmsnorm(ms, norm_mlp_post[i])
    k_prompt_new = jnp.stack(kp_new)
    v_prompt_new = jnp.stack(vp_new)
    conv_prompt_new = jnp.stack(cp_new)
    k_sample_new = jnp.stack(ks_new)
    v_sample_new = jnp.stack(vs_new)
    conv_sample_new = jnp.stack(cs_new)
    return (xp, xs, k_prompt_new, v_prompt_new, conv_prompt_new, k_sample_new, v_sample_new, conv_sample_new)
```

```python
import functools

import jax
import jax.numpy as jnp
from jax import lax
from jax.experimental import pallas as pl
from jax.experimental.pallas import tpu as pltpu

F32 = jnp.float32
BF16 = jnp.bfloat16

RMS_EPS = 1e-6
VMEM_LIMIT_BYTES = 56 * 1024 * 1024
LANES = 128
HEADS_PER_GROUP = 2
ATT_BLOCK = 256
PAGES_PER_STEP = 4
CONV_HALO = 8


def _cparams(*sem):
    return pltpu.CompilerParams(dimension_semantics=sem, vmem_limit_bytes=VMEM_LIMIT_BYTES)


def _const_spec(shape):
    nd = len(shape)
    return pl.BlockSpec(shape, lambda *_: (0,) * nd, pipeline_mode=pl.Buffered(1))


def _rms(x, g):
    inv = lax.rsqrt(jnp.mean(x * x, axis=-1, keepdims=True) + RMS_EPS)
    return (x * inv) * g


def _softplus(z):
    return jnp.maximum(z, 0.0) + jnp.log(1.0 + jnp.exp(-jnp.abs(z)))


def _qkv_kernel(x_ref, g_ref, w_ref, q_ref, k_ref, v_ref, *, d, scale):
    xn = _rms(x_ref[...], g_ref[...]).astype(BF16)
    qkv = jnp.dot(xn, w_ref[...], preferred_element_type=F32)
    q_ref[...] = (qkv[:, :d] * scale).astype(BF16)
    k_ref[...] = qkv[:, d:2 * d]
    v_ref[...] = qkv[:, 2 * d:]


def _qkv_prompt_kernel(x_ref, g_ref, wqk_ref, wkvt_ref, q_ref, kb_ref, kt_ref, vt_ref, vtb_ref, *, d, scale):
    xn = _rms(x_ref[...], g_ref[...]).astype(BF16)
    qk = jnp.dot(xn, wqk_ref[...], preferred_element_type=F32)
    q_ref[...] = (qk[:, :d] * scale).astype(BF16)
    kb_ref[...] = qk[:, d:].astype(BF16)
    kvt = lax.dot_general(wkvt_ref[...], xn, (((1,), (1,)), ((), ())), preferred_element_type=F32)
    kt_ref[...] = kvt[:d]
    vt_ref[...] = kvt[d:]
    vtb = kvt[d:].astype(BF16)
    for c in range(vtb_ref.shape[0]):
        vtb_ref[c] = vtb[:, c * ATT_BLOCK:(c + 1) * ATT_BLOCK]


def _qkv_proj_sample(x, g, w_bf, scale):
    m, d = x.shape
    n = w_bf.shape[1]
    blk = pl.BlockSpec((m, d), lambda i: (0, 0))
    return pl.pallas_call(
        functools.partial(_qkv_kernel, d=d, scale=scale),
        grid=(1,),
        in_specs=[blk, _const_spec((1, d)), _const_spec((d, n))],
        out_specs=[blk, blk, blk],
        out_shape=[jax.ShapeDtypeStruct((m, d), BF16),
                   jax.ShapeDtypeStruct((m, d), F32),
                   jax.ShapeDtypeStruct((m, d), F32)],
        compiler_params=_cparams("arbitrary"),
        name="qkv_proj_sample",
    )(x, g, w_bf)


def _qkv_proj_prompt(x, g, wqk_bf, wkvt_bf, scale, *, batch, seq, tm):
    m, d = x.shape
    nb = seq // tm
    cb = tm // ATT_BLOCK
    row = lambda b, i: (b * nb + i, 0)
    row_blk = pl.BlockSpec((tm, d), row)
    t_blk = pl.BlockSpec((None, d, tm), lambda b, i: (b, 0, i))
    return pl.pallas_call(
        functools.partial(_qkv_prompt_kernel, d=d, scale=scale),
        grid=(batch, nb),
        in_specs=[row_blk, _const_spec((1, d)), _const_spec((d, 2 * d)), _const_spec((2 * d, d))],
        out_specs=[row_blk, row_blk, t_blk, t_blk,
                   pl.BlockSpec((cb, d, ATT_BLOCK), lambda b, i: (b * nb + i, 0, 0))],
        out_shape=[jax.ShapeDtypeStruct((m, d), BF16),
                   jax.ShapeDtypeStruct((m, d), BF16),
                   jax.ShapeDtypeStruct((batch, d, seq), F32),
                   jax.ShapeDtypeStruct((batch, d, seq), F32),
                   jax.ShapeDtypeStruct((m // ATT_BLOCK, d, ATT_BLOCK), BF16)],
        compiler_params=_cparams("parallel", "parallel"),
        name="qkv_proj_prompt",
    )(x, g, wqk_bf, wkvt_bf)


def _convin_kernel(x_ref, g_ref, w_ref, gb_ref, u_ref, *, d):
    xn = _rms(x_ref[...], g_ref[...]).astype(BF16)
    y = jnp.dot(xn, w_ref[...], preferred_element_type=F32)
    gb_ref[...] = y[:, :d]
    u_ref[...] = y[:, d:2 * d] * y[:, 2 * d:]


def _convin_proj(x, g, w_bf, *, tm, name):
    m, d = x.shape
    n = w_bf.shape[1]
    row = lambda i: (i, 0)
    return pl.pallas_call(
        functools.partial(_convin_kernel, d=d),
        grid=(m // tm,),
        in_specs=[pl.BlockSpec((tm, d), row), _const_spec((1, d)), _const_spec((d, n))],
        out_specs=[pl.BlockSpec((tm, d), row)] * 2,
        out_shape=[jax.ShapeDtypeStruct((m, d), F32)] * 2,
        compiler_params=_cparams("parallel"),
        name=name,
    )(x, g, w_bf)


def _post_body(a_bf, x, wo_ref, gpost_ref, gpre_ref, wup_ref, wdown_ref, gmpost_ref):
    o = jnp.dot(a_bf, wo_ref[...], preferred_element_type=F32)
    x1 = x + _rms(o, gpost_ref[...])
    hn = _rms(x1, gpre_ref[...]).astype(BF16)
    h = jnp.dot(hn, wup_ref[...], preferred_element_type=F32)
    h = jnp.square(jnp.maximum(h, 0.0)).astype(BF16)
    mlp = jnp.dot(h, wdown_ref[...], preferred_element_type=F32)
    return x1 + _rms(mlp, gmpost_ref[...])


def _post_attn_kernel(a_ref, x_ref, wo_ref, gpost_ref, gpre_ref, wup_ref, wdown_ref, gmpost_ref, o_ref):
    o_ref[...] = _post_body(a_ref[...], x_ref[...], wo_ref, gpost_ref, gpre_ref,
                            wup_ref, wdown_ref, gmpost_ref)


def _post_conv_prompt_kernel(gb_ref, u_ref, halo_ref, cw_ref, x_ref, wo_ref, gpost_ref, gpre_ref,
                             wup_ref, wdown_ref, gmpost_ref, o_ref, *, blocks_per_seq):
    u = u_ref[...]
    seq_start = (pl.program_id(0) % blocks_per_seq) == 0
    halo = jnp.where(seq_start, 0.0, halo_ref[...])
    r = lax.broadcasted_iota(jnp.int32, u.shape, 0)
    u1 = jnp.where(r == 0, halo[CONV_HALO - 1:CONV_HALO, :], pltpu.roll(u, 1, axis=0))
    u2 = pltpu.roll(u, 2, axis=0)
    u2 = jnp.where(r == 0, halo[CONV_HALO - 2:CONV_HALO - 1, :], u2)
    u2 = jnp.where(r == 1, halo[CONV_HALO - 1:CONV_HALO, :], u2)
    cw = cw_ref[...]
    y = cw[0:1, :] * u2 + cw[1:2, :] * u1 + cw[2:3, :] * u
    a = (gb_ref[...] * y).astype(BF16)
    o_ref[...] = _post_body(a, x_ref[...], wo_ref, gpost_ref, gpre_ref, wup_ref, wdown_ref, gmpost_ref)


def _post_conv_sample_kernel(gb_ref, u_ref, s0_ref, s1_ref, cw_ref, x_ref, wo_ref, gpost_ref, gpre_ref,
                             wup_ref, wdown_ref, gmpost_ref, o_ref):
    cw = cw_ref[...]
    y = cw[0:1, :] * s0_ref[...] + cw[1:2, :] * s1_ref[...] + cw[2:3, :] * u_ref[...]
    a = (gb_ref[...] * y).astype(BF16)
    o_ref[...] = _post_body(a, x_ref[...], wo_ref, gpost_ref, gpre_ref, wup_ref, wdown_ref, gmpost_ref)


def _post_weight_specs(d, ff):
    return [_const_spec((d, d)), _const_spec((1, d)), _const_spec((1, d)),
            _const_spec((d, ff)), _const_spec((ff, d)), _const_spec((1, d))]


def _post_attn(a, x, weights, *, tm, name):
    m, d = x.shape
    ff = weights[3].shape[1]
    row = lambda i: (i, 0)
    blk = pl.BlockSpec((tm, d), row)
    return pl.pallas_call(
        _post_attn_kernel,
        grid=(m // tm,),
        in_specs=[blk, blk] + _post_weight_specs(d, ff),
        out_specs=blk,
        out_shape=jax.ShapeDtypeStruct((m, d), F32),
        compiler_params=_cparams("parallel"),
        name=name,
    )(a, x, *weights)


def _post_conv_prompt(gb, u, cw, x, weights, *, tm, seq):
    m, d = x.shape
    ff = weights[3].shape[1]
    row = lambda i: (i, 0)
    blk = pl.BlockSpec((tm, d), row)
    halo_blocks = tm // CONV_HALO
    halo = pl.BlockSpec((CONV_HALO, d), lambda i: (jnp.maximum(i * halo_blocks - 1, 0), 0))
    return pl.pallas_call(
        functools.partial(_post_conv_prompt_kernel, blocks_per_seq=seq // tm),
        grid=(m // tm,),
        in_specs=[blk, blk, halo, _const_spec((3, d)), blk] + _post_weight_specs(d, ff),
        out_specs=blk,
        out_shape=jax.ShapeDtypeStruct((m, d), F32),
        compiler_params=_cparams("parallel"),
        name="post_conv_prompt",
    )(gb, u, u, cw, x, *weights)


def _post_conv_sample(gb, u, s0, s1, cw, x, weights):
    m, d = x.shape
    ff = weights[3].shape[1]
    blk = pl.BlockSpec((m, d), lambda i: (0, 0))
    return pl.pallas_call(
        _post_conv_sample_kernel,
        grid=(1,),
        in_specs=[blk, blk, blk, blk, _const_spec((3, d)), blk] + _post_weight_specs(d, ff),
        out_specs=blk,
        out_shape=jax.ShapeDtypeStruct((m, d), F32),
        compiler_params=_cparams("arbitrary"),
        name="post_conv_sample",
    )(gb, u, s0, s1, cw, x, *weights)


def _attn_kernel(bias_ref, q_ref, k_ref, vt_ref, tri_ref, o_ref, *, head_dim):
    hp = pl.program_id(1)
    i = pl.program_id(2)
    blk = ATT_BLOCK
    q2 = q_ref[...]
    lane = lax.broadcasted_iota(jnp.int32, q2.shape, 1)
    zero = jnp.zeros_like(q2)
    q_heads = [jnp.where(lane < head_dim, q2, zero), jnp.where(lane >= head_dim, q2, zero)]
    biases = [bias_ref[hp * HEADS_PER_GROUP + h] for h in range(HEADS_PER_GROUP)]
    tri = tri_ref[...]
    key_idx = lax.broadcasted_iota(jnp.int32, (blk, blk), 0)
    qry_idx = lax.broadcasted_iota(jnp.int32, (blk, blk), 1)
    causal = key_idx < qry_idx

    def block(j, h, carry, acc, diagonal):
        kb = k_ref[pl.ds(pl.multiple_of(j * blk, blk), blk), :]
        zt = lax.dot_general(kb, q_heads[h], (((1,), (1,)), ((), ())),
                             preferred_element_type=F32) + biases[h]
        sp = _softplus(zt)
        if diagonal:
            sp = jnp.where(causal, sp, 0.0)
        suf = jnp.dot(tri, sp.astype(BF16), preferred_element_type=F32)
        w = jnp.exp(zt - suf)
        if diagonal:
            w = jnp.where(causal, w, 0.0)
        vt = vt_ref[j, h * head_dim:(h + 1) * head_dim, :]
        ob = jnp.dot(vt, w.astype(BF16), preferred_element_type=F32)
        acc = acc + ob * jnp.exp(-carry)
        carry = carry + suf[0:1, :]
        return carry, acc

    state = []
    for h in range(HEADS_PER_GROUP):
        state.extend(block(i, h, jnp.zeros((1, blk), F32), jnp.zeros((head_dim, blk), F32), True))

    def body(jj, st):
        j = i - 1 - jj
        out = []
        for h in range(HEADS_PER_GROUP):
            out.extend(block(j, h, st[2 * h], st[2 * h + 1], False))
        return tuple(out)

    state = lax.fori_loop(0, i, body, tuple(state))
    ot = jnp.concatenate([state[2 * h + 1] for h in range(HEADS_PER_GROUP)], axis=0)
    o_ref[...] = ot.T.astype(BF16)


def _prompt_attention(q_bf, k_bf, vt_bf, bias, tri, *, batch, seq, head_dim):
    m, d = q_bf.shape
    blk = ATT_BLOCK
    nq = seq // blk
    groups = d // LANES
    return pl.pallas_call(
        functools.partial(_attn_kernel, head_dim=head_dim),
        grid=(batch, groups, nq),
        in_specs=[
            pl.BlockSpec(memory_space=pltpu.SMEM),
            pl.BlockSpec((blk, LANES), lambda b, g, i: (b * nq + i, g)),
            pl.BlockSpec((seq, LANES), lambda b, g, i: (b, g)),
            pl.BlockSpec((nq, LANES, blk), lambda b, g, i: (b, g, 0)),
            _const_spec((blk, blk)),
        ],
        out_specs=pl.BlockSpec((blk, LANES), lambda b, g, i: (b * nq + i, g)),
        out_shape=jax.ShapeDtypeStruct((m, d), BF16),
        compiler_params=_cparams("parallel", "parallel", "arbitrary"),
        name="prompt_attention",
    )(bias, q_bf, k_bf, vt_bf, tri)


def _decode_kernel(pt_ref, qb_ref, q_ref, knew_ref, vnew_ref, bias_ref, tri_ref, *refs,
                   n_pages, page_size, dec_seq):
    del pt_ref
    pps = PAGES_PER_STEP
    k_refs = refs[:pps]
    v_refs = refs[pps:2 * pps]
    o_ref = refs[2 * pps]
    acc_ref, onew_ref, carry_ref = refs[2 * pps + 1:]
    step = pl.program_id(1)
    past_len = n_pages * page_size
    n_heads = acc_ref.shape[0]
    bias = bias_ref[...]

    @pl.when(step == 0)
    def _():
        q_pos = past_len + dec_seq - 1
        k_pos = past_len + lax.broadcasted_iota(jnp.int32, (n_heads, dec_seq), 1)
        visible = k_pos < q_pos
        z = jnp.sum(q_ref[...] * knew_ref[...], axis=1, keepdims=True) + bias
        sp = jnp.where(visible, _softplus(z), 0.0)
        w = jnp.where(visible, jnp.exp(z - sp), 0.0)
        onew_ref[...] = w * vnew_ref[...]
        carry_ref[...] = sp
        acc_ref[...] = jnp.zeros_like(acc_ref)

    qb = qb_ref[...]
    z = [jnp.sum(k_refs[p][...] * qb, axis=1) + bias for p in range(pps)]
    sp = jnp.concatenate([_softplus(zp) for zp in z], axis=0)
    sp_hi = sp.astype(BF16)
    sp_lo = (sp - sp_hi.astype(F32)).astype(BF16)
    tri = tri_ref[...]
    suf = (jnp.dot(sp_hi, tri, preferred_element_type=F32)
           + jnp.dot(sp_lo, tri, preferred_element_type=F32))
    carry = carry_ref[...]
    acc = acc_ref[...]
    for p in reversed(range(pps)):
        suf_p = suf[p * n_heads:(p + 1) * n_heads, :]
        w = jnp.exp(z[p] - suf_p - carry)
        acc = acc + v_refs[p][...] * w[:, None, :]
        carry = carry + suf_p[:, 0:1]
    acc_ref[...] = acc
    carry_ref[...] = carry

    @pl.when(step == pl.num_programs(1) - 1)
    def _():
        o_ref[...] = jnp.sum(acc_ref[...], axis=2) + onew_ref[...]


def _decode_attention(q, k_new, v_new, cache_kt, cache_vt, layer, page_table, bias):
    db, n_heads, head_dim = q.shape
    n_pages = page_table.shape[1]
    page_size = cache_kt.shape[-1]
    pps = PAGES_PER_STEP
    steps = n_pages // pps
    q_lanes = jnp.broadcast_to(q[..., None], (db, n_heads, head_dim, page_size))
    tri = (lax.broadcasted_iota(jnp.int32, (page_size, page_size), 0)
           >= lax.broadcasted_iota(jnp.int32, (page_size, page_size), 1)).astype(BF16)

    def page_spec(p):
        return pl.BlockSpec((None, None, n_heads, head_dim, page_size),
                            lambda b, s, pt: (layer, pt[b, (steps - 1 - s) * pps + p], 0, 0, 0))

    per_seq = pl.BlockSpec((None, n_heads, head_dim), lambda b, s, pt: (b, 0, 0))
    grid_spec = pltpu.PrefetchScalarGridSpec(
        num_scalar_prefetch=1,
        grid=(db, steps),
        in_specs=[
            pl.BlockSpec((None, n_heads, head_dim, page_size), lambda b, s, pt: (b, 0, 0, 0)),
            per_seq, per_seq, per_seq,
            pl.BlockSpec((n_heads, 1), lambda b, s, pt: (0, 0)),
            pl.BlockSpec((page_size, page_size), lambda b, s, pt: (0, 0)),
        ] + [page_spec(p) for p in range(pps)] * 2,
        out_specs=per_seq,
        scratch_shapes=[pltpu.VMEM((n_heads, head_dim, page_size), F32),
                        pltpu.VMEM((n_heads, head_dim), F32),
                        pltpu.VMEM((n_heads, 1), F32)],
    )
    return pl.pallas_call(
        functools.partial(_decode_kernel, n_pages=n_pages, page_size=page_size, dec_seq=1),
        grid_spec=grid_spec,
        out_shape=jax.ShapeDtypeStruct((db, n_heads, head_dim), F32),
        compiler_params=_cparams("parallel", "arbitrary"),
        name="decode_attention",
    )(page_table, q_lanes, q, k_new, v_new, bias.reshape(n_heads, 1), tri,
      *([cache_kt] * pps), *([cache_vt] * pps))


def kernel(x_prompt, x_sample, cache_k, cache_v, state_conv, page_table, norm_mix_pre, norm_mix_post,
           norm_mlp_pre, norm_mlp_post, w_qkv, sb_bias, w_attn_out, w_conv_in, conv_w, w_conv_out,
           w_mlp_up, w_mlp_down):
    batch, seq, d = x_prompt.shape
    db, ts, _ = x_sample.shape
    assert ts == 1, "the sample group decodes one token per sequence"
    depth = norm_mix_pre.shape[0]
    n_heads = sb_bias.shape[1]
    head_dim = d // n_heads
    assert head_dim * HEADS_PER_GROUP == LANES
    assert seq % ATT_BLOCK == 0 and page_table.shape[1] % PAGES_PER_STEP == 0
    scale = head_dim ** -0.5
    tm = 512 if seq % 512 == 0 else ATT_BLOCK
    tm_post = 256

    xp = x_prompt.reshape(batch * seq, d)
    xs = x_sample.reshape(db * ts, d)
    tri = (lax.broadcasted_iota(jnp.int32, (ATT_BLOCK, ATT_BLOCK), 1)
           >= lax.broadcasted_iota(jnp.int32, (ATT_BLOCK, ATT_BLOCK), 0)).astype(BF16)
    cache_kt = jnp.transpose(cache_k, (0, 1, 3, 4, 2))
    cache_vt = jnp.transpose(cache_v, (0, 1, 3, 4, 2))

    row = lambda g, i: g[i].reshape(1, d)
    heads_major = lambda t: jnp.transpose(t.reshape(batch, n_heads, head_dim, seq), (0, 3, 1, 2))
    kp_new, vp_new, cp_new, ks_new, vs_new, cs_new = [], [], [], [], [], []
    for i in range(depth):
        post_w = [None, row(norm_mix_post, i), row(norm_mlp_pre, i), w_mlp_up[i].astype(BF16),
                  w_mlp_down[i].astype(BF16), row(norm_mlp_post, i)]
        g_pre = row(norm_mix_pre, i)
        if i % 2 == 0:
            a = i // 2
            w_bf = w_qkv[a].astype(BF16)
            wkvt_bf = w_qkv[a][:, d:].T.astype(BF16)
            post_w[0] = w_attn_out[a].astype(BF16)
            q_bf, k_bf, kt, vt, vt_bf = _qkv_proj_prompt(xp, g_pre, w_bf[:, :2 * d], wkvt_bf, scale,
                                                         batch=batch, seq=seq, tm=tm)
            o_bf = _prompt_attention(q_bf, k_bf, vt_bf, sb_bias[a], tri,
                                     batch=batch, seq=seq, head_dim=head_dim)
            xp = _post_attn(o_bf, xp, post_w, tm=tm_post, name="post_attn_prompt")
            qs_bf, ks, vs = _qkv_proj_sample(xs, g_pre, w_bf, scale)
            heads = lambda t: t.reshape(db, n_heads, head_dim)
            os_ = _decode_attention(heads(qs_bf.astype(F32)), heads(ks), heads(vs), cache_kt, cache_vt,
                                    a, page_table, sb_bias[a])
            xs = _post_attn(os_.reshape(db, d).astype(BF16), xs, post_w, tm=db * ts, name="post_attn_sample")
            kp_new.append(heads_major(kt))
            vp_new.append(heads_major(vt))
            ks_new.append(ks.reshape(db, ts, n_heads, head_dim))
            vs_new.append(vs.reshape(db, ts, n_heads, head_dim))
        else:
            c = i // 2
            w_bf = w_conv_in[c].astype(BF16)
            post_w[0] = w_conv_out[c].astype(BF16)
            gb, u = _convin_proj(xp, g_pre, w_bf, tm=tm, name="convin_prompt")
            xp = _post_conv_prompt(gb, u, conv_w[c], xp, post_w, tm=tm_post, seq=seq)
            cp_new.append(u.reshape(batch, seq, d)[:, seq - 2:, :])
            gbs, us = _convin_proj(xs, g_pre, w_bf, tm=db * ts, name="convin_sample")
            st = state_conv[c]
            xs = _post_conv_sample(gbs, us, st[:, 0, :], st[:, 1, :], conv_w[c], xs, post_w)
            cs_new.append(jnp.concatenate([st[:, 1:, :], us.reshape(db, ts, d)], axis=1))
    return (xp.reshape(batch, seq, d), xs.reshape(db, ts, d),
            jnp.stack(kp_new), jnp.stack(vp_new), jnp.stack(cp_new),
            jnp.stack(ks_new), jnp.stack(vs_new), jnp.stack(cs_new))
```

```python
import functools

import jax
import jax.numpy as jnp
from jax import lax
from jax.experimental import pallas as pl
from jax.experimental.pallas import tpu as pltpu

F32 = jnp.float32
BF16 = jnp.bfloat16

RMS_EPS = 1e-6
VMEM_LIMIT_BYTES = 56 * 1024 * 1024
LANES = 128
HEADS_PER_GROUP = 2
ATT_BLOCK = 256
ATT_UNROLL = 4
LOG2E = 1.4426950408889634
PAGES_PER_STEP = 8
CONV_HALO = 8


def _cparams(*sem):
    return pltpu.CompilerParams(dimension_semantics=sem, vmem_limit_bytes=VMEM_LIMIT_BYTES)


def _const_spec(shape):
    nd = len(shape)
    return pl.BlockSpec(shape, lambda *_: (0,) * nd, pipeline_mode=pl.Buffered(1))


def _rms(x, g):
    inv = lax.rsqrt(jnp.mean(x * x, axis=-1, keepdims=True) + RMS_EPS)
    return (x * inv) * g


def _softplus(z):
    return jnp.maximum(z, 0.0) + jnp.log(1.0 + jnp.exp(-jnp.abs(z)))


def _qkv_kernel(x_ref, g_ref, w_ref, q_ref, k_ref, v_ref, *, d, scale):
    xn = _rms(x_ref[...], g_ref[...]).astype(BF16)
    qkv = jnp.dot(xn, w_ref[...], preferred_element_type=F32)
    q_ref[...] = (qkv[:, :d] * scale).astype(BF16)
    k_ref[...] = qkv[:, d:2 * d]
    v_ref[...] = qkv[:, 2 * d:]


def _qkv_prompt_kernel(x_ref, g_ref, wk_ref, wt_ref, kb_ref, qtb_ref, kt_ref, vt_ref, vtb_ref, *, d, scale):
    xn = _rms(x_ref[...], g_ref[...]).astype(BF16)
    kb_ref[...] = jnp.dot(xn, wk_ref[...], preferred_element_type=F32).astype(BF16)
    t = lax.dot_general(wt_ref[...], xn, (((1,), (1,)), ((), ())), preferred_element_type=F32)
    kt_ref[...] = t[d:2 * d]
    vt_ref[...] = t[2 * d:]
    qtb = (t[:d] * scale).astype(BF16)
    vtb = t[2 * d:].astype(BF16)
    for c in range(vtb_ref.shape[0]):
        qtb_ref[c] = qtb[:, c * ATT_BLOCK:(c + 1) * ATT_BLOCK]
        vtb_ref[c] = vtb[:, c * ATT_BLOCK:(c + 1) * ATT_BLOCK]


def _qkv_proj_sample(x, g, w_bf, scale):
    m, d = x.shape
    n = w_bf.shape[1]
    blk = pl.BlockSpec((m, d), lambda i: (0, 0))
    return pl.pallas_call(
        functools.partial(_qkv_kernel, d=d, scale=scale),
        grid=(1,),
        in_specs=[blk, _const_spec((1, d)), _const_spec((d, n))],
        out_specs=[blk, blk, blk],
        out_shape=[jax.ShapeDtypeStruct((m, d), BF16),
                   jax.ShapeDtypeStruct((m, d), F32),
                   jax.ShapeDtypeStruct((m, d), F32)],
        compiler_params=_cparams("arbitrary"),
        name="qkv_proj_sample",
    )(x, g, w_bf)


def _qkv_proj_prompt(x, g, wk_bf, wt_bf, scale, *, batch, seq, tm):
    m, d = x.shape
    nb = seq // tm
    cb = tm // ATT_BLOCK
    row_blk = pl.BlockSpec((tm, d), lambda b, i: (b * nb + i, 0))
    t_blk = pl.BlockSpec((None, d, tm), lambda b, i: (b, 0, i))
    tb_blk = pl.BlockSpec((cb, d, ATT_BLOCK), lambda b, i: (b * nb + i, 0, 0))
    tb_shape = jax.ShapeDtypeStruct((m // ATT_BLOCK, d, ATT_BLOCK), BF16)
    return pl.pallas_call(
        functools.partial(_qkv_prompt_kernel, d=d, scale=scale),
        grid=(batch, nb),
        in_specs=[row_blk, _const_spec((1, d)), _const_spec((d, d)), _const_spec((3 * d, d))],
        out_specs=[row_blk, tb_blk, t_blk, t_blk, tb_blk],
        out_shape=[jax.ShapeDtypeStruct((m, d), BF16),
                   tb_shape,
                   jax.ShapeDtypeStruct((batch, d, seq), F32),
                   jax.ShapeDtypeStruct((batch, d, seq), F32),
                   tb_shape],
        compiler_params=_cparams("parallel", "parallel"),
        name="qkv_proj_prompt",
    )(x, g, wk_bf, wt_bf)


def _convin_kernel(x_ref, g_ref, w_ref, gb_ref, u_ref, *, d):
    xn = _rms(x_ref[...], g_ref[...]).astype(BF16)
    y = jnp.dot(xn, w_ref[...], preferred_element_type=F32)
    gb_ref[...] = y[:, :d]
    u_ref[...] = y[:, d:2 * d] * y[:, 2 * d:]


def _convin_proj(x, g, w_bf, *, tm, name):
    m, d = x.shape
    n = w_bf.shape[1]
    row = lambda i: (i, 0)
    return pl.pallas_call(
        functools.partial(_convin_kernel, d=d),
        grid=(m // tm,),
        in_specs=[pl.BlockSpec((tm, d), row), _const_spec((1, d)), _const_spec((d, n))],
        out_specs=[pl.BlockSpec((tm, d), row)] * 2,
        out_shape=[jax.ShapeDtypeStruct((m, d), F32)] * 2,
        compiler_params=_cparams("parallel"),
        name=name,
    )(x, g, w_bf)


def _post_body(a_bf, x, wo_ref, gpost_ref, gpre_ref, wup_ref, wdown_ref, gmpost_ref):
    o = jnp.dot(a_bf, wo_ref[...], preferred_element_type=F32)
    x1 = x + _rms(o, gpost_ref[...])
    hn = _rms(x1, gpre_ref[...]).astype(BF16)
    h = jnp.dot(hn, wup_ref[...], preferred_element_type=F32)
    h = jnp.square(jnp.maximum(h, 0.0)).astype(BF16)
    mlp = jnp.dot(h, wdown_ref[...], preferred_element_type=F32)
    return x1 + _rms(mlp, gmpost_ref[...])


def _post_attn_kernel(a_ref, x_ref, wo_ref, gpost_ref, gpre_ref, wup_ref, wdown_ref, gmpost_ref, o_ref):
    o_ref[...] = _post_body(a_ref[...], x_ref[...], wo_ref, gpost_ref, gpre_ref,
                            wup_ref, wdown_ref, gmpost_ref)


def _post_conv_prompt_kernel(gb_ref, u_ref, halo_ref, cw_ref, x_ref, wo_ref, gpost_ref, gpre_ref,
                             wup_ref, wdown_ref, gmpost_ref, o_ref, *, blocks_per_seq):
    u = u_ref[...]
    seq_start = (pl.program_id(0) % blocks_per_seq) == 0
    halo = jnp.where(seq_start, 0.0, halo_ref[...])
    r = lax.broadcasted_iota(jnp.int32, u.shape, 0)
    u1 = jnp.where(r == 0, halo[CONV_HALO - 1:CONV_HALO, :], pltpu.roll(u, 1, axis=0))
    u2 = pltpu.roll(u, 2, axis=0)
    u2 = jnp.where(r == 0, halo[CONV_HALO - 2:CONV_HALO - 1, :], u2)
    u2 = jnp.where(r == 1, halo[CONV_HALO - 1:CONV_HALO, :], u2)
    cw = cw_ref[...]
    y = cw[0:1, :] * u2 + cw[1:2, :] * u1 + cw[2:3, :] * u
    a = (gb_ref[...] * y).astype(BF16)
    o_ref[...] = _post_body(a, x_ref[...], wo_ref, gpost_ref, gpre_ref, wup_ref, wdown_ref, gmpost_ref)


def _post_conv_sample_kernel(gb_ref, u_ref, s0_ref, s1_ref, cw_ref, x_ref, wo_ref, gpost_ref, gpre_ref,
                             wup_ref, wdown_ref, gmpost_ref, o_ref):
    cw = cw_ref[...]
    y = cw[0:1, :] * s0_ref[...] + cw[1:2, :] * s1_ref[...] + cw[2:3, :] * u_ref[...]
    a = (gb_ref[...] * y).astype(BF16)
    o_ref[...] = _post_body(a, x_ref[...], wo_ref, gpost_ref, gpre_ref, wup_ref, wdown_ref, gmpost_ref)


def _post_weight_specs(d, ff):
    return [_const_spec((d, d)), _const_spec((1, d)), _const_spec((1, d)),
            _const_spec((d, ff)), _const_spec((ff, d)), _const_spec((1, d))]


def _post_attn(a, x, weights, *, tm, name):
    m, d = x.shape
    ff = weights[3].shape[1]
    row = lambda i: (i, 0)
    blk = pl.BlockSpec((tm, d), row)
    return pl.pallas_call(
        _post_attn_kernel,
        grid=(m // tm,),
        in_specs=[blk, blk] + _post_weight_specs(d, ff),
        out_specs=blk,
        out_shape=jax.ShapeDtypeStruct((m, d), F32),
        compiler_params=_cparams("parallel"),
        name=name,
    )(a, x, *weights)


def _post_conv_prompt(gb, u, cw, x, weights, *, tm, seq):
    m, d = x.shape
    ff = weights[3].shape[1]
    row = lambda i: (i, 0)
    blk = pl.BlockSpec((tm, d), row)
    halo_blocks = tm // CONV_HALO
    halo = pl.BlockSpec((CONV_HALO, d), lambda i: (jnp.maximum(i * halo_blocks - 1, 0), 0))
    return pl.pallas_call(
        functools.partial(_post_conv_prompt_kernel, blocks_per_seq=seq // tm),
        grid=(m // tm,),
        in_specs=[blk, blk, halo, _const_spec((3, d)), blk] + _post_weight_specs(d, ff),
        out_specs=blk,
        out_shape=jax.ShapeDtypeStruct((m, d), F32),
        compiler_params=_cparams("parallel"),
        name="post_conv_prompt",
    )(gb, u, u, cw, x, *weights)


def _post_conv_sample(gb, u, s0, s1, cw, x, weights):
    m, d = x.shape
    ff = weights[3].shape[1]
    blk = pl.BlockSpec((m, d), lambda i: (0, 0))
    return pl.pallas_call(
        _post_conv_sample_kernel,
        grid=(1,),
        in_specs=[blk, blk, blk, blk, _const_spec((3, d)), blk] + _post_weight_specs(d, ff),
        out_specs=blk,
        out_shape=jax.ShapeDtypeStruct((m, d), F32),
        compiler_params=_cparams("arbitrary"),
        name="post_conv_sample",
    )(gb, u, s0, s1, cw, x, *weights)


def _neg_abs(x):
    bits = lax.bitcast_convert_type(x, jnp.uint32) | jnp.uint32(0x80000000)
    return lax.bitcast_convert_type(bits, F32)


def _softplus2(z2):
    return jnp.maximum(z2, 0.0) + jnp.log(1.0 + jnp.exp2(_neg_abs(z2))) * LOG2E


def _attn_kernel(qi_ref, kj_ref, bias_ref, k_ref, qt_ref, vt_ref, tri_ref, o_ref, acc_ref, carry_ref,
                 *, head_dim, diag_unroll, off_unroll):
    hp = pl.program_id(1)
    blk = ATT_BLOCK
    nq = acc_ref.shape[0]
    n_off = nq * (nq - 1) // 2
    biases = [bias_ref[hp * HEADS_PER_GROUP + h] * LOG2E for h in range(HEADS_PER_GROUP)]

    def chains(visits, diagonal):
        if diagonal:
            key_idx = lax.broadcasted_iota(jnp.int32, (blk, blk), 0)
            qry_idx = lax.broadcasted_iota(jnp.int32, (blk, blk), 1)
            causal = key_idx < qry_idx
        z2s = []
        for qi, kj, h in visits:
            qt = qt_ref[qi]
            feat = lax.broadcasted_iota(jnp.int32, qt.shape, 0)
            own = (feat < head_dim) if h == 0 else (feat >= head_dim)
            qt_h = jnp.where(own, qt, jnp.zeros_like(qt))
            kb = k_ref[pl.ds(pl.multiple_of(kj * blk, blk), blk), :]
            z2s.append(jnp.dot(kb, qt_h, preferred_element_type=F32) + biases[h])
        sufs = []
        for z2 in z2s:
            sp = _softplus2(z2)
            if diagonal:
                sp = jnp.where(causal, sp, 0.0)
            sufs.append(jnp.dot(tri_ref[...], sp.astype(BF16), preferred_element_type=F32))
        outs = []
        for (qi, kj, h), z2, suf in zip(visits, z2s, sufs):
            w = jnp.exp2(z2 - suf)
            if diagonal:
                w = jnp.where(causal, w, 0.0)
            vt = vt_ref[kj, h * head_dim:(h + 1) * head_dim, :]
            ob = jnp.dot(vt, w.astype(BF16), preferred_element_type=F32)
            outs.append((ob, suf[0:1, :]))
        return outs

    def diag_body(it, c):
        visits = [(it * diag_unroll + u, it * diag_unroll + u, h)
                  for u in range(diag_unroll) for h in range(HEADS_PER_GROUP)]
        for (qi, _, h), (ob, tot) in zip(visits, chains(visits, True)):
            acc_ref[qi, h] = ob
            carry_ref[qi, h] = tot
        return c

    lax.fori_loop(0, nq // diag_unroll, diag_body, 0)

    def off_body(it, c):
        visits = [(qi_ref[it * off_unroll + u], kj_ref[it * off_unroll + u], h)
                  for u in range(off_unroll) for h in range(HEADS_PER_GROUP)]
        for (qi, _, h), (ob, tot) in zip(visits, chains(visits, False)):
            carry = carry_ref[qi, h]
            acc_ref[qi, h] = acc_ref[qi, h] + ob * jnp.exp2(-carry)
            carry_ref[qi, h] = carry + tot
        return c

    lax.fori_loop(0, n_off // off_unroll, off_body, 0)

    def out_body(qi, c):
        ot = jnp.concatenate([acc_ref[qi, h] for h in range(HEADS_PER_GROUP)], axis=0)
        o_ref[pl.ds(pl.multiple_of(qi * blk, blk), blk), :] = ot.T.astype(BF16)
        return c

    lax.fori_loop(0, nq, out_body, 0)


def _largest_divisor(n, cap):
    return max(u for u in range(1, cap + 1) if n % u == 0)


def _prompt_attention(k_bf, qt_bf, vt_bf, bias, tri, *, batch, seq, head_dim):
    m, d = k_bf.shape
    blk = ATT_BLOCK
    nq = seq // blk
    assert nq >= 2
    groups = d // LANES
    pairs = [(qi, kj) for qi in range(1, nq) for kj in range(qi - 1, -1, -1)]
    qi_tab = jnp.asarray([p[0] for p in pairs], jnp.int32)
    kj_tab = jnp.asarray([p[1] for p in pairs], jnp.int32)
    blocked = pl.BlockSpec((nq, LANES, blk), lambda b, g, qi, kj: (b, g, 0))
    seq_cols = pl.BlockSpec((seq, LANES), lambda b, g, qi, kj: (b, g))
    grid_spec = pltpu.PrefetchScalarGridSpec(
        num_scalar_prefetch=2,
        grid=(batch, groups),
        in_specs=[
            pl.BlockSpec(memory_space=pltpu.SMEM),
            seq_cols, blocked, blocked,
            pl.BlockSpec((blk, blk), lambda b, g, qi, kj: (0, 0)),
        ],
        out_specs=seq_cols,
        scratch_shapes=[pltpu.VMEM((nq, HEADS_PER_GROUP, head_dim, blk), F32),
                        pltpu.VMEM((nq, HEADS_PER_GROUP, 1, blk), F32)],
    )
    return pl.pallas_call(
        functools.partial(_attn_kernel, head_dim=head_dim,
                          diag_unroll=_largest_divisor(nq, ATT_UNROLL),
                          off_unroll=_largest_divisor(len(pairs), ATT_UNROLL)),
        grid_spec=grid_spec,
        out_shape=jax.ShapeDtypeStruct((m, d), BF16),
        compiler_params=_cparams("parallel", "parallel"),
        name="prompt_attention",
    )(qi_tab, kj_tab, bias, k_bf, qt_bf, vt_bf, tri)


def _decode_kernel(pt_ref, qb_ref, q_ref, knew_ref, vnew_ref, bias_ref, tri_ref, *refs,
                   n_pages, page_size, dec_seq):
    del pt_ref
    pps = PAGES_PER_STEP
    k_refs = refs[:pps]
    v_refs = refs[pps:2 * pps]
    o_ref = refs[2 * pps]
    acc_ref, onew_ref, carry_ref = refs[2 * pps + 1:]
    step = pl.program_id(1)
    past_len = n_pages * page_size
    n_heads = acc_ref.shape[0]
    bias = bias_ref[...]

    @pl.when(step == 0)
    def _():
        q_pos = past_len + dec_seq - 1
        k_pos = past_len + lax.broadcasted_iota(jnp.int32, (n_heads, dec_seq), 1)
        visible = k_pos < q_pos
        z = jnp.sum(q_ref[...] * knew_ref[...], axis=1, keepdims=True) + bias
        sp = jnp.where(visible, _softplus(z), 0.0)
        w = jnp.where(visible, jnp.exp(z - sp), 0.0)
        onew_ref[...] = w * vnew_ref[...]
        carry_ref[...] = sp
        acc_ref[...] = jnp.zeros_like(acc_ref)

    qb = qb_ref[...]
    z = [jnp.sum(k_refs[p][...] * qb, axis=1) + bias for p in range(pps)]
    sp = jnp.concatenate([_softplus(zp) for zp in z], axis=0)
    sp_hi = sp.astype(BF16)
    sp_lo = (sp - sp_hi.astype(F32)).astype(BF16)
    tri = tri_ref[...]
    suf = (jnp.dot(sp_hi, tri, preferred_element_type=F32)
           + jnp.dot(sp_lo, tri, preferred_element_type=F32))
    carry = carry_ref[...]
    acc = acc_ref[...]
    for p in reversed(range(pps)):
        suf_p = suf[p * n_heads:(p + 1) * n_heads, :]
        w = jnp.exp(z[p] - suf_p - carry)
        acc = acc + v_refs[p][...] * w[:, None, :]
        carry = carry + suf_p[:, 0:1]
    acc_ref[...] = acc
    carry_ref[...] = carry

    @pl.when(step == pl.num_programs(1) - 1)
    def _():
        o_ref[...] = jnp.sum(acc_ref[...], axis=2) + onew_ref[...]


def _decode_attention(q, k_new, v_new, cache_kt, cache_vt, layer, page_table, bias):
    db, n_heads, head_dim = q.shape
    n_pages = page_table.shape[1]
    page_size = cache_kt.shape[-1]
    pps = PAGES_PER_STEP
    steps = n_pages // pps
    q_lanes = jnp.broadcast_to(q[..., None], (db, n_heads, head_dim, page_size))
    tri = (lax.broadcasted_iota(jnp.int32, (page_size, page_size), 0)
           >= lax.broadcasted_iota(jnp.int32, (page_size, page_size), 1)).astype(BF16)

    def page_spec(p):
        return pl.BlockSpec((None, None, n_heads, head_dim, page_size),
                            lambda b, s, pt: (layer, pt[b, (steps - 1 - s) * pps + p], 0, 0, 0))

    per_seq = pl.BlockSpec((None, n_heads, head_dim), lambda b, s, pt: (b, 0, 0))
    grid_spec = pltpu.PrefetchScalarGridSpec(
        num_scalar_prefetch=1,
        grid=(db, steps),
        in_specs=[
            pl.BlockSpec((None, n_heads, head_dim, page_size), lambda b, s, pt: (b, 0, 0, 0)),
            per_seq, per_seq, per_seq,
            pl.BlockSpec((n_heads, 1), lambda b, s, pt: (0, 0)),
            pl.BlockSpec((page_size, page_size), lambda b, s, pt: (0, 0)),
        ] + [page_spec(p) for p in range(pps)] * 2,
        out_specs=per_seq,
        scratch_shapes=[pltpu.VMEM((n_heads, head_dim, page_size), F32),
                        pltpu.VMEM((n_heads, head_dim), F32),
                        pltpu.VMEM((n_heads, 1), F32)],
    )
    return pl.pallas_call(
        functools.partial(_decode_kernel, n_pages=n_pages, page_size=page_size, dec_seq=1),
        grid_spec=grid_spec,
        out_shape=jax.ShapeDtypeStruct((db, n_heads, head_dim), F32),
        compiler_params=_cparams("parallel", "arbitrary"),
        name="decode_attention",
    )(page_table, q_lanes, q, k_new, v_new, bias.reshape(n_heads, 1), tri,
      *([cache_kt] * pps), *([cache_vt] * pps))


def kernel(x_prompt, x_sample, cache_k, cache_v, state_conv, page_table, norm_mix_pre, norm_mix_post,
           norm_mlp_pre, norm_mlp_post, w_qkv, sb_bias, w_attn_out, w_conv_in, conv_w, w_conv_out,
           w_mlp_up, w_mlp_down):
    batch, seq, d = x_prompt.shape
    db, ts, _ = x_sample.shape
    assert ts == 1, "the sample group decodes one token per sequence"
    depth = norm_mix_pre.shape[0]
    n_heads = sb_bias.shape[1]
    head_dim = d // n_heads
    assert head_dim * HEADS_PER_GROUP == LANES
    assert seq % ATT_BLOCK == 0 and page_table.shape[1] % PAGES_PER_STEP == 0
    scale = head_dim ** -0.5
    tm = 512 if seq % 512 == 0 else ATT_BLOCK
    tm_post = 256

    xp = x_prompt.reshape(batch * seq, d)
    xs = x_sample.reshape(db * ts, d)
    tri = (lax.broadcasted_iota(jnp.int32, (ATT_BLOCK, ATT_BLOCK), 1)
           >= lax.broadcasted_iota(jnp.int32, (ATT_BLOCK, ATT_BLOCK), 0)).astype(BF16)
    cache_kt = jnp.transpose(cache_k, (0, 1, 3, 4, 2))
    cache_vt = jnp.transpose(cache_v, (0, 1, 3, 4, 2))

    row = lambda g, i: g[i].reshape(1, d)
    heads_major = lambda t: jnp.transpose(t.reshape(batch, n_heads, head_dim, seq), (0, 3, 1, 2))
    kp_new, vp_new, cp_new, ks_new, vs_new, cs_new = [], [], [], [], [], []
    for i in range(depth):
        post_w = [None, row(norm_mix_post, i), row(norm_mlp_pre, i), w_mlp_up[i].astype(BF16),
                  w_mlp_down[i].astype(BF16), row(norm_mlp_post, i)]
        g_pre = row(norm_mix_pre, i)
        if i % 2 == 0:
            a = i // 2
            w_bf = w_qkv[a].astype(BF16)
            wt_bf = w_qkv[a].T.astype(BF16)
            post_w[0] = w_attn_out[a].astype(BF16)
            k_bf, qt_bf, kt, vt, vt_bf = _qkv_proj_prompt(xp, g_pre, w_bf[:, d:2 * d], wt_bf, scale * LOG2E,
                                                          batch=batch, seq=seq, tm=tm)
            o_bf = _prompt_attention(k_bf, qt_bf, vt_bf, sb_bias[a], tri,
                                     batch=batch, seq=seq, head_dim=head_dim)
            xp = _post_attn(o_bf, xp, post_w, tm=tm_post, name="post_attn_prompt")
            qs_bf, ks, vs = _qkv_proj_sample(xs, g_pre, w_bf, scale)
            heads = lambda t: t.reshape(db, n_heads, head_dim)
            os_ = _decode_attention(heads(qs_bf.astype(F32)), heads(ks), heads(vs), cache_kt, cache_vt,
                                    a, page_table, sb_bias[a])
            xs = _post_attn(os_.reshape(db, d).astype(BF16), xs, post_w, tm=db * ts, name="post_attn_sample")
            kp_new.append(heads_major(kt))
            vp_new.append(heads_major(vt))
            ks_new.append(ks.reshape(db, ts, n_heads, head_dim))
            vs_new.append(vs.reshape(db, ts, n_heads, head_dim))
        else:
            c = i // 2
            w_bf = w_conv_in[c].astype(BF16)
            post_w[0] = w_conv_out[c].astype(BF16)
            gb, u = _convin_proj(xp, g_pre, w_bf, tm=tm, name="convin_prompt")
            xp = _post_conv_prompt(gb, u, conv_w[c], xp, post_w, tm=tm_post, seq=seq)
            cp_new.append(u.reshape(batch, seq, d)[:, seq - 2:, :])
            gbs, us = _convin_proj(xs, g_pre, w_bf, tm=db * ts, name="convin_sample")
            st = state_conv[c]
            xs = _post_conv_sample(gbs, us, st[:, 0, :], st[:, 1, :], conv_w[c], xs, post_w)
            cs_new.append(jnp.concatenate([st[:, 1:, :], us.reshape(db, ts, d)], axis=1))
    return (xp.reshape(batch, seq, d), xs.reshape(db, ts, d),
            jnp.stack(kp_new), jnp.stack(vp_new), jnp.stack(cp_new),
            jnp.stack(ks_new), jnp.stack(vs_new), jnp.stack(cs_new))
```

```python
import functools

import jax
import jax.numpy as jnp
from jax import lax
from jax.experimental import pallas as pl
from jax.experimental.pallas import tpu as pltpu

F32 = jnp.float32
BF16 = jnp.bfloat16

RMS_EPS = 1e-6
VMEM_LIMIT_BYTES = 56 * 1024 * 1024
LANES = 128
HEADS_PER_GROUP = 2
ATT_BLOCK = 256
ATT_UNROLL = 4
LOG2E = 1.4426950408889634
PAGES_PER_STEP = 8
CONV_HALO = 8


def _cparams(*sem):
    return pltpu.CompilerParams(dimension_semantics=sem, vmem_limit_bytes=VMEM_LIMIT_BYTES)


def _const_spec(shape):
    nd = len(shape)
    return pl.BlockSpec(shape, lambda *_: (0,) * nd, pipeline_mode=pl.Buffered(1))


def _rms(x, g):
    inv = lax.rsqrt(jnp.mean(x * x, axis=-1, keepdims=True) + RMS_EPS)
    return (x * inv) * g


def _softplus(z):
    return jnp.maximum(z, 0.0) + jnp.log(1.0 + jnp.exp(-jnp.abs(z)))


def _qkv_kernel(x_ref, g_ref, w_ref, q_ref, k_ref, v_ref, *, d, scale):
    xn = _rms(x_ref[...], g_ref[...]).astype(BF16)
    qkv = jnp.dot(xn, w_ref[...], preferred_element_type=F32)
    q_ref[...] = (qkv[:, :d] * scale).astype(BF16)
    k_ref[...] = qkv[:, d:2 * d]
    v_ref[...] = qkv[:, 2 * d:]


def _qkv_prompt_kernel(x_ref, g_ref, wk_ref, wt_ref, kb_ref, qtb_ref, kt_ref, vt_ref, vtb_ref, *, d, scale):
    xn = _rms(x_ref[...], g_ref[...]).astype(BF16)
    kb_ref[...] = jnp.dot(xn, wk_ref[...], preferred_element_type=F32).astype(BF16)
    t = lax.dot_general(wt_ref[...], xn, (((1,), (1,)), ((), ())), preferred_element_type=F32)
    kt_ref[...] = t[d:2 * d]
    vt_ref[...] = t[2 * d:]
    qtb = (t[:d] * scale).astype(BF16)
    vtb = t[2 * d:].astype(BF16)
    for c in range(vtb_ref.shape[0]):
        qtb_ref[c] = qtb[:, c * ATT_BLOCK:(c + 1) * ATT_BLOCK]
        vtb_ref[c] = vtb[:, c * ATT_BLOCK:(c + 1) * ATT_BLOCK]


def _qkv_proj_sample(x, g, w_bf, scale):
    m, d = x.shape
    n = w_bf.shape[1]
    blk = pl.BlockSpec((m, d), lambda i: (0, 0))
    return pl.pallas_call(
        functools.partial(_qkv_kernel, d=d, scale=scale),
        grid=(1,),
        in_specs=[blk, _const_spec((1, d)), _const_spec((d, n))],
        out_specs=[blk, blk, blk],
        out_shape=[jax.ShapeDtypeStruct((m, d), BF16),
                   jax.ShapeDtypeStruct((m, d), F32),
                   jax.ShapeDtypeStruct((m, d), F32)],
        compiler_params=_cparams("arbitrary"),
        name="qkv_proj_sample",
    )(x, g, w_bf)


def _qkv_proj_prompt(x, g, wk_bf, wt_bf, scale, *, batch, seq, tm):
    m, d = x.shape
    nb = seq // tm
    cb = tm // ATT_BLOCK
    row_blk = pl.BlockSpec((tm, d), lambda b, i: (b * nb + i, 0))
    t_blk = pl.BlockSpec((None, d, tm), lambda b, i: (b, 0, i))
    tb_blk = pl.BlockSpec((cb, d, ATT_BLOCK), lambda b, i: (b * nb + i, 0, 0))
    tb_shape = jax.ShapeDtypeStruct((m // ATT_BLOCK, d, ATT_BLOCK), BF16)
    return pl.pallas_call(
        functools.partial(_qkv_prompt_kernel, d=d, scale=scale),
        grid=(batch, nb),
        in_specs=[row_blk, _const_spec((1, d)), _const_spec((d, d)), _const_spec((3 * d, d))],
        out_specs=[row_blk, tb_blk, t_blk, t_blk, tb_blk],
        out_shape=[jax.ShapeDtypeStruct((m, d), BF16),
                   tb_shape,
                   jax.ShapeDtypeStruct((batch, d, seq), F32),
                   jax.ShapeDtypeStruct((batch, d, seq), F32),
                   tb_shape],
        compiler_params=_cparams("parallel", "parallel"),
        name="qkv_proj_prompt",
    )(x, g, wk_bf, wt_bf)


def _convin_kernel(x_ref, g_ref, w_ref, gb_ref, u_ref, *, d):
    xn = _rms(x_ref[...], g_ref[...]).astype(BF16)
    y = jnp.dot(xn, w_ref[...], preferred_element_type=F32)
    gb_ref[...] = y[:, :d]
    u_ref[...] = y[:, d:2 * d] * y[:, 2 * d:]


def _convin_proj(x, g, w_bf, *, tm, name):
    m, d = x.shape
    n = w_bf.shape[1]
    row = lambda i: (i, 0)
    return pl.pallas_call(
        functools.partial(_convin_kernel, d=d),
        grid=(m // tm,),
        in_specs=[pl.BlockSpec((tm, d), row), _const_spec((1, d)), _const_spec((d, n))],
        out_specs=[pl.BlockSpec((tm, d), row)] * 2,
        out_shape=[jax.ShapeDtypeStruct((m, d), F32)] * 2,
        compiler_params=_cparams("parallel"),
        name=name,
    )(x, g, w_bf)


def _post_body(a_bf, x, wo_ref, gpost_ref, gpre_ref, wup_ref, wdown_ref, gmpost_ref):
    o = jnp.dot(a_bf, wo_ref[...], preferred_element_type=F32)
    x1 = x + _rms(o, gpost_ref[...])
    hn = _rms(x1, gpre_ref[...]).astype(BF16)
    h = jnp.dot(hn, wup_ref[...], preferred_element_type=F32)
    h = jnp.square(jnp.maximum(h, 0.0)).astype(BF16)
    mlp = jnp.dot(h, wdown_ref[...], preferred_element_type=F32)
    return x1 + _rms(mlp, gmpost_ref[...])


def _post_attn_kernel(a_ref, x_ref, wo_ref, gpost_ref, gpre_ref, wup_ref, wdown_ref, gmpost_ref, o_ref):
    o_ref[...] = _post_body(a_ref[...], x_ref[...], wo_ref, gpost_ref, gpre_ref,
                            wup_ref, wdown_ref, gmpost_ref)


def _post_conv_prompt_kernel(gb_ref, u_ref, halo_ref, cw_ref, x_ref, wo_ref, gpost_ref, gpre_ref,
                             wup_ref, wdown_ref, gmpost_ref, o_ref, *, blocks_per_seq):
    u = u_ref[...]
    seq_start = (pl.program_id(0) % blocks_per_seq) == 0
    halo = jnp.where(seq_start, 0.0, halo_ref[...])
    r = lax.broadcasted_iota(jnp.int32, u.shape, 0)
    u1 = jnp.where(r == 0, halo[CONV_HALO - 1:CONV_HALO, :], pltpu.roll(u, 1, axis=0))
    u2 = pltpu.roll(u, 2, axis=0)
    u2 = jnp.where(r == 0, halo[CONV_HALO - 2:CONV_HALO - 1, :], u2)
    u2 = jnp.where(r == 1, halo[CONV_HALO - 1:CONV_HALO, :], u2)
    cw = cw_ref[...]
    y = cw[0:1, :] * u2 + cw[1:2, :] * u1 + cw[2:3, :] * u
    a = (gb_ref[...] * y).astype(BF16)
    o_ref[...] = _post_body(a, x_ref[...], wo_ref, gpost_ref, gpre_ref, wup_ref, wdown_ref, gmpost_ref)


def _post_conv_sample_kernel(gb_ref, u_ref, s0_ref, s1_ref, cw_ref, x_ref, wo_ref, gpost_ref, gpre_ref,
                             wup_ref, wdown_ref, gmpost_ref, o_ref):
    cw = cw_ref[...]
    y = cw[0:1, :] * s0_ref[...] + cw[1:2, :] * s1_ref[...] + cw[2:3, :] * u_ref[...]
    a = (gb_ref[...] * y).astype(BF16)
    o_ref[...] = _post_body(a, x_ref[...], wo_ref, gpost_ref, gpre_ref, wup_ref, wdown_ref, gmpost_ref)


def _post_weight_specs(d, ff):
    return [_const_spec((d, d)), _const_spec((1, d)), _const_spec((1, d)),
            _const_spec((d, ff)), _const_spec((ff, d)), _const_spec((1, d))]


def _post_attn(a, x, weights, *, tm, name):
    m, d = x.shape
    ff = weights[3].shape[1]
    row = lambda i: (i, 0)
    blk = pl.BlockSpec((tm, d), row)
    return pl.pallas_call(
        _post_attn_kernel,
        grid=(m // tm,),
        in_specs=[blk, blk] + _post_weight_specs(d, ff),
        out_specs=blk,
        out_shape=jax.ShapeDtypeStruct((m, d), F32),
        compiler_params=_cparams("parallel"),
        name=name,
    )(a, x, *weights)


def _post_conv_prompt(gb, u, cw, x, weights, *, tm, seq):
    m, d = x.shape
    ff = weights[3].shape[1]
    row = lambda i: (i, 0)
    blk = pl.BlockSpec((tm, d), row)
    halo_blocks = tm // CONV_HALO
    halo = pl.BlockSpec((CONV_HALO, d), lambda i: (jnp.maximum(i * halo_blocks - 1, 0), 0))
    return pl.pallas_call(
        functools.partial(_post_conv_prompt_kernel, blocks_per_seq=seq // tm),
        grid=(m // tm,),
        in_specs=[blk, blk, halo, _const_spec((3, d)), blk] + _post_weight_specs(d, ff),
        out_specs=blk,
        out_shape=jax.ShapeDtypeStruct((m, d), F32),
        compiler_params=_cparams("parallel"),
        name="post_conv_prompt",
    )(gb, u, u, cw, x, *weights)


def _post_conv_sample(gb, u, s0, s1, cw, x, weights):
    m, d = x.shape
    ff = weights[3].shape[1]
    blk = pl.BlockSpec((m, d), lambda i: (0, 0))
    return pl.pallas_call(
        _post_conv_sample_kernel,
        grid=(1,),
        in_specs=[blk, blk, blk, blk, _const_spec((3, d)), blk] + _post_weight_specs(d, ff),
        out_specs=blk,
        out_shape=jax.ShapeDtypeStruct((m, d), F32),
        compiler_params=_cparams("arbitrary"),
        name="post_conv_sample",
    )(gb, u, s0, s1, cw, x, *weights)


def _softplus2(z2):
    return jnp.maximum(z2, 0.0) + jnp.log(1.0 + jnp.exp2(-jnp.abs(z2))) * LOG2E


def _attn_kernel(qi_ref, kj_ref, bias_ref, k_ref, qt_ref, vt_ref, tri_ref, o_ref,
                 acc_ref, carry_ref, z_even_ref, z_odd_ref, *, head_dim, unroll):
    hp = pl.program_id(1)
    blk = ATT_BLOCK
    nq = acc_ref.shape[0]
    diag_trips = nq // unroll
    n_trips = (nq * (nq + 1) // 2) // unroll
    biases = [bias_ref[hp * HEADS_PER_GROUP + h] * LOG2E for h in range(HEADS_PER_GROUP)]
    z_refs = (z_even_ref, z_odd_ref)

    def visits_of(trip):
        return [(qi_ref[trip * unroll + u], kj_ref[trip * unroll + u], h)
                for u in range(unroll) for h in range(HEADS_PER_GROUP)]

    def scores(trip, z_ref):
        for c, (qi, kj, h) in enumerate(visits_of(trip)):
            qt = qt_ref[qi]
            feat = lax.broadcasted_iota(jnp.int32, qt.shape, 0)
            own = (feat < head_dim) if h == 0 else (feat >= head_dim)
            qt_h = jnp.where(own, qt, jnp.zeros_like(qt))
            kb = k_ref[pl.ds(pl.multiple_of(kj * blk, blk), blk), :]
            z_ref[c] = jnp.dot(kb, qt_h, preferred_element_type=F32) + biases[h]

    def finish(trip, z_ref, diagonal):
        visits = visits_of(trip)
        if diagonal:
            key_idx = lax.broadcasted_iota(jnp.int32, (blk, blk), 0)
            qry_idx = lax.broadcasted_iota(jnp.int32, (blk, blk), 1)
            causal = key_idx < qry_idx
        sufs = []
        for c in range(len(visits)):
            sp = _softplus2(z_ref[c])
            if diagonal:
                sp = jnp.where(causal, sp, 0.0)
            sufs.append(jnp.dot(tri_ref[...], sp.astype(BF16), preferred_element_type=F32))
        outs = []
        for c, (qi, kj, h) in enumerate(visits):
            w = jnp.exp2(z_ref[c] - sufs[c])
            if diagonal:
                w = jnp.where(causal, w, 0.0)
            vt = vt_ref[kj, h * head_dim:(h + 1) * head_dim, :]
            outs.append(jnp.dot(vt, w.astype(BF16), preferred_element_type=F32))
        for (qi, kj, h), ob, suf in zip(visits, outs, sufs):
            tot = suf[0:1, :]
            if diagonal:
                acc_ref[qi, h] = ob
                carry_ref[qi, h] = tot
            else:
                carry = carry_ref[qi, h]
                acc_ref[qi, h] = acc_ref[qi, h] + ob * jnp.exp2(-carry)
                carry_ref[qi, h] = carry + tot

    def trip_body(trip, parity, diagonal):
        scores(jnp.minimum(trip + 1, n_trips - 1), z_refs[1 - parity])
        finish(trip, z_refs[parity], diagonal)

    def run(first, count, diagonal):
        if count and first % 2:
            trip_body(first, 1, diagonal)
            first, count = first + 1, count - 1

        def pair(p, c):
            trip_body(first + 2 * p, 0, diagonal)
            trip_body(first + 2 * p + 1, 1, diagonal)
            return c

        lax.fori_loop(0, count // 2, pair, 0)
        if count % 2:
            trip_body(first + count - 1, 0, diagonal)

    scores(0, z_even_ref)
    run(0, diag_trips, True)
    run(diag_trips, n_trips - diag_trips, False)

    def out_body(qi, c):
        ot = jnp.concatenate([acc_ref[qi, h] for h in range(HEADS_PER_GROUP)], axis=0)
        o_ref[pl.ds(pl.multiple_of(qi * blk, blk), blk), :] = ot.T.astype(BF16)
        return c

    lax.fori_loop(0, nq, out_body, 0)


def _prompt_attention(k_bf, qt_bf, vt_bf, bias, tri, *, batch, seq, head_dim):
    m, d = k_bf.shape
    blk = ATT_BLOCK
    nq = seq // blk
    assert nq >= 2
    groups = d // LANES
    pairs = [(qi, qi) for qi in range(nq)]
    pairs += [(qi, kj) for qi in range(1, nq) for kj in range(qi - 1, -1, -1)]
    unroll = max(u for u in range(1, ATT_UNROLL + 1) if nq % u == 0 and len(pairs) % u == 0)
    qi_tab = jnp.asarray([p[0] for p in pairs], jnp.int32)
    kj_tab = jnp.asarray([p[1] for p in pairs], jnp.int32)
    blocked = pl.BlockSpec((nq, LANES, blk), lambda b, g, qi, kj: (b, g, 0))
    seq_cols = pl.BlockSpec((seq, LANES), lambda b, g, qi, kj: (b, g))
    z_tiles = pltpu.VMEM((unroll * HEADS_PER_GROUP, blk, blk), F32)
    grid_spec = pltpu.PrefetchScalarGridSpec(
        num_scalar_prefetch=2,
        grid=(batch, groups),
        in_specs=[
            pl.BlockSpec(memory_space=pltpu.SMEM),
            seq_cols, blocked, blocked,
            pl.BlockSpec((blk, blk), lambda b, g, qi, kj: (0, 0)),
        ],
        out_specs=seq_cols,
        scratch_shapes=[pltpu.VMEM((nq, HEADS_PER_GROUP, head_dim, blk), F32),
                        pltpu.VMEM((nq, HEADS_PER_GROUP, 1, blk), F32),
                        z_tiles, z_tiles],
    )
    return pl.pallas_call(
        functools.partial(_attn_kernel, head_dim=head_dim, unroll=unroll),
        grid_spec=grid_spec,
        out_shape=jax.ShapeDtypeStruct((m, d), BF16),
        compiler_params=_cparams("parallel", "parallel"),
        name="prompt_attention",
    )(qi_tab, kj_tab, bias, k_bf, qt_bf, vt_bf, tri)


def _decode_kernel(pt_ref, qb_ref, q_ref, knew_ref, vnew_ref, bias_ref, tri_ref, *refs,
                   n_pages, page_size, dec_seq):
    del pt_ref
    pps = PAGES_PER_STEP
    k_refs = refs[:pps]
    v_refs = refs[pps:2 * pps]
    o_ref = refs[2 * pps]
    acc_ref, onew_ref, carry_ref = refs[2 * pps + 1:]
    step = pl.program_id(1)
    past_len = n_pages * page_size
    n_heads = acc_ref.shape[0]
    bias = bias_ref[...]

    @pl.when(step == 0)
    def _():
        q_pos = past_len + dec_seq - 1
        k_pos = past_len + lax.broadcasted_iota(jnp.int32, (n_heads, dec_seq), 1)
        visible = k_pos < q_pos
        z = jnp.sum(q_ref[...] * knew_ref[...], axis=1, keepdims=True) + bias
        sp = jnp.where(visible, _softplus(z), 0.0)
        w = jnp.where(visible, jnp.exp(z - sp), 0.0)
        onew_ref[...] = w * vnew_ref[...]
        carry_ref[...] = sp
        acc_ref[...] = jnp.zeros_like(acc_ref)

    qb = qb_ref[...]
    z = [jnp.sum(k_refs[p][...] * qb, axis=1) + bias for p in range(pps)]
    sp = jnp.concatenate([_softplus(zp) for zp in z], axis=0)
    sp_hi = sp.astype(BF16)
    sp_lo = (sp - sp_hi.astype(F32)).astype(BF16)
    tri = tri_ref[...]
    suf = (jnp.dot(sp_hi, tri, preferred_element_type=F32)
           + jnp.dot(sp_lo, tri, preferred_element_type=F32))
    carry = carry_ref[...]
    acc = acc_ref[...]
    for p in reversed(range(pps)):
        suf_p = suf[p * n_heads:(p + 1) * n_heads, :]
        w = jnp.exp(z[p] - suf_p - carry)
        acc = acc + v_refs[p][...] * w[:, None, :]
        carry = carry + suf_p[:, 0:1]
    acc_ref[...] = acc
    carry_ref[...] = carry

    @pl.when(step == pl.num_programs(1) - 1)
    def _():
        o_ref[...] = jnp.sum(acc_ref[...], axis=2) + onew_ref[...]


def _decode_attention(q, k_new, v_new, cache_kt, cache_vt, layer, page_table, bias):
    db, n_heads, head_dim = q.shape
    n_pages = page_table.shape[1]
    page_size = cache_kt.shape[-1]
    pps = PAGES_PER_STEP
    steps = n_pages // pps
    q_lanes = jnp.broadcast_to(q[..., None], (db, n_heads, head_dim, page_size))
    tri = (lax.broadcasted_iota(jnp.int32, (page_size, page_size), 0)
           >= lax.broadcasted_iota(jnp.int32, (page_size, page_size), 1)).astype(BF16)

    def page_spec(p):
        return pl.BlockSpec((None, None, n_heads, head_dim, page_size),
                            lambda b, s, pt: (layer, pt[b, (steps - 1 - s) * pps + p], 0, 0, 0))

    per_seq = pl.BlockSpec((None, n_heads, head_dim), lambda b, s, pt: (b, 0, 0))
    grid_spec = pltpu.PrefetchScalarGridSpec(
        num_scalar_prefetch=1,
        grid=(db, steps),
        in_specs=[
            pl.BlockSpec((None, n_heads, head_dim, page_size), lambda b, s, pt: (b, 0, 0, 0)),
            per_seq, per_seq, per_seq,
            pl.BlockSpec((n_heads, 1), lambda b, s, pt: (0, 0)),
            pl.BlockSpec((page_size, page_size), lambda b, s, pt: (0, 0)),
        ] + [page_spec(p) for p in range(pps)] * 2,
        out_specs=per_seq,
        scratch_shapes=[pltpu.VMEM((n_heads, head_dim, page_size), F32),
                        pltpu.VMEM((n_heads, head_dim), F32),
                        pltpu.VMEM((n_heads, 1), F32)],
    )
    return pl.pallas_call(
        functools.partial(_decode_kernel, n_pages=n_pages, page_size=page_size, dec_seq=1),
        grid_spec=grid_spec,
        out_shape=jax.ShapeDtypeStruct((db, n_heads, head_dim), F32),
        compiler_params=_cparams("parallel", "arbitrary"),
        name="decode_attention",
    )(page_table, q_lanes, q, k_new, v_new, bias.reshape(n_heads, 1), tri,
      *([cache_kt] * pps), *([cache_vt] * pps))


def kernel(x_prompt, x_sample, cache_k, cache_v, state_conv, page_table, norm_mix_pre, norm_mix_post,
           norm_mlp_pre, norm_mlp_post, w_qkv, sb_bias, w_attn_out, w_conv_in, conv_w, w_conv_out,
           w_mlp_up, w_mlp_down):
    batch, seq, d = x_prompt.shape
    db, ts, _ = x_sample.shape
    assert ts == 1, "the sample group decodes one token per sequence"
    depth = norm_mix_pre.shape[0]
    n_heads = sb_bias.shape[1]
    head_dim = d // n_heads
    assert head_dim * HEADS_PER_GROUP == LANES
    assert seq % ATT_BLOCK == 0 and page_table.shape[1] % PAGES_PER_STEP == 0
    scale = head_dim ** -0.5
    tm = 512 if seq % 512 == 0 else ATT_BLOCK
    tm_post = 256

    xp = x_prompt.reshape(batch * seq, d)
    xs = x_sample.reshape(db * ts, d)
    tri = (lax.broadcasted_iota(jnp.int32, (ATT_BLOCK, ATT_BLOCK), 1)
           >= lax.broadcasted_iota(jnp.int32, (ATT_BLOCK, ATT_BLOCK), 0)).astype(BF16)
    cache_kt = jnp.transpose(cache_k, (0, 1, 3, 4, 2))
    cache_vt = jnp.transpose(cache_v, (0, 1, 3, 4, 2))

    row = lambda g, i: g[i].reshape(1, d)
    heads_major = lambda t: jnp.transpose(t.reshape(batch, n_heads, head_dim, seq), (0, 3, 1, 2))
    kp_new, vp_new, cp_new, ks_new, vs_new, cs_new = [], [], [], [], [], []
    for i in range(depth):
        post_w = [None, row(norm_mix_post, i), row(norm_mlp_pre, i), w_mlp_up[i].astype(BF16),
                  w_mlp_down[i].astype(BF16), row(norm_mlp_post, i)]
        g_pre = row(norm_mix_pre, i)
        if i % 2 == 0:
            a = i // 2
            w_bf = w_qkv[a].astype(BF16)
            wt_bf = w_qkv[a].T.astype(BF16)
            post_w[0] = w_attn_out[a].astype(BF16)
            k_bf, qt_bf, kt, vt, vt_bf = _qkv_proj_prompt(xp, g_pre, w_bf[:, d:2 * d], wt_bf, scale * LOG2E,
                                                          batch=batch, seq=seq, tm=tm)
            o_bf = _prompt_attention(k_bf, qt_bf, vt_bf, sb_bias[a], tri,
                                     batch=batch, seq=seq, head_dim=head_dim)
            xp = _post_attn(o_bf, xp, post_w, tm=tm_post, name="post_attn_prompt")
            qs_bf, ks, vs = _qkv_proj_sample(xs, g_pre, w_bf, scale)
            heads = lambda t: t.reshape(db, n_heads, head_dim)
            os_ = _decode_attention(heads(qs_bf.astype(F32)), heads(ks), heads(vs), cache_kt, cache_vt,
                                    a, page_table, sb_bias[a])
            xs = _post_attn(os_.reshape(db, d).astype(BF16), xs, post_w, tm=db * ts, name="post_attn_sample")
            kp_new.append(heads_major(kt))
            vp_new.append(heads_major(vt))
            ks_new.append(ks.reshape(db, ts, n_heads, head_dim))
            vs_new.append(vs.reshape(db, ts, n_heads, head_dim))
        else:
            c = i // 2
            w_bf = w_conv_in[c].astype(BF16)
            post_w[0] = w_conv_out[c].astype(BF16)
            gb, u = _convin_proj(xp, g_pre, w_bf, tm=tm, name="convin_prompt")
            xp = _post_conv_prompt(gb, u, conv_w[c], xp, post_w, tm=tm_post, seq=seq)
            cp_new.append(u.reshape(batch, seq, d)[:, seq - 2:, :])
            gbs, us = _convin_proj(xs, g_pre, w_bf, tm=db * ts, name="convin_sample")
            st = state_conv[c]
            xs = _post_conv_sample(gbs, us, st[:, 0, :], st[:, 1, :], conv_w[c], xs, post_w)
            cs_new.append(jnp.concatenate([st[:, 1:, :], us.reshape(db, ts, d)], axis=1))
    return (xp.reshape(batch, seq, d), xs.reshape(db, ts, d),
            jnp.stack(kp_new), jnp.stack(vp_new), jnp.stack(cp_new),
            jnp.stack(ks_new), jnp.stack(vs_new), jnp.stack(cs_new))
```

```python
import functools

import jax
import jax.numpy as jnp
from jax import lax
from jax.experimental import pallas as pl
from jax.experimental.pallas import tpu as pltpu

F32 = jnp.float32
BF16 = jnp.bfloat16

RMS_EPS = 1e-6
VMEM_LIMIT_BYTES = 56 * 1024 * 1024
LANES = 128
HEADS_PER_GROUP = 2
ATT_BLOCK = 256
ATT_UNROLL = 4
LOG2E = 1.4426950408889634
PAGES_PER_STEP = 8
CONV_HALO = 8


def _cparams(*sem):
    return pltpu.CompilerParams(dimension_semantics=sem, vmem_limit_bytes=VMEM_LIMIT_BYTES)


def _const_spec(shape):
    nd = len(shape)
    return pl.BlockSpec(shape, lambda *_: (0,) * nd, pipeline_mode=pl.Buffered(1))


def _rms(x, g):
    inv = lax.rsqrt(jnp.mean(x * x, axis=-1, keepdims=True) + RMS_EPS)
    return (x * inv) * g


def _softplus(z):
    return jnp.maximum(z, 0.0) + jnp.log(1.0 + jnp.exp(-jnp.abs(z)))


def _qkv_kernel(x_ref, g_ref, w_ref, q_ref, k_ref, v_ref, *, d, scale):
    xn = _rms(x_ref[...], g_ref[...]).astype(BF16)
    qkv = jnp.dot(xn, w_ref[...], preferred_element_type=F32)
    q_ref[...] = (qkv[:, :d] * scale).astype(BF16)
    k_ref[...] = qkv[:, d:2 * d]
    v_ref[...] = qkv[:, 2 * d:]


def _qkv_prompt_kernel(x_ref, g_ref, wk_ref, wt_ref, kb_ref, qtb_ref, kt_ref, vt_ref, vtb_ref, *, d, scale):
    xn = _rms(x_ref[...], g_ref[...]).astype(BF16)
    kb_ref[...] = jnp.dot(xn, wk_ref[...], preferred_element_type=F32).astype(BF16)
    t = lax.dot_general(wt_ref[...], xn, (((1,), (1,)), ((), ())), preferred_element_type=F32)
    kt_ref[...] = t[d:2 * d]
    vt_ref[...] = t[2 * d:]
    qtb = (t[:d] * scale).astype(BF16)
    vtb = t[2 * d:].astype(BF16)
    for c in range(vtb_ref.shape[0]):
        qtb_ref[c] = qtb[:, c * ATT_BLOCK:(c + 1) * ATT_BLOCK]
        vtb_ref[c] = vtb[:, c * ATT_BLOCK:(c + 1) * ATT_BLOCK]


def _qkv_proj_sample(x, g, w_bf, scale):
    m, d = x.shape
    n = w_bf.shape[1]
    blk = pl.BlockSpec((m, d), lambda i: (0, 0))
    return pl.pallas_call(
        functools.partial(_qkv_kernel, d=d, scale=scale),
        grid=(1,),
        in_specs=[blk, _const_spec((1, d)), _const_spec((d, n))],
        out_specs=[blk, blk, blk],
        out_shape=[jax.ShapeDtypeStruct((m, d), BF16),
                   jax.ShapeDtypeStruct((m, d), F32),
                   jax.ShapeDtypeStruct((m, d), F32)],
        compiler_params=_cparams("arbitrary"),
        name="qkv_proj_sample",
    )(x, g, w_bf)


def _qkv_proj_prompt(x, g, wk_bf, wt_bf, scale, *, batch, seq, tm):
    m, d = x.shape
    nb = seq // tm
    cb = tm // ATT_BLOCK
    row_blk = pl.BlockSpec((tm, d), lambda b, i: (b * nb + i, 0))
    t_blk = pl.BlockSpec((None, d, tm), lambda b, i: (b, 0, i))
    tb_blk = pl.BlockSpec((cb, d, ATT_BLOCK), lambda b, i: (b * nb + i, 0, 0))
    tb_shape = jax.ShapeDtypeStruct((m // ATT_BLOCK, d, ATT_BLOCK), BF16)
    return pl.pallas_call(
        functools.partial(_qkv_prompt_kernel, d=d, scale=scale),
        grid=(batch, nb),
        in_specs=[row_blk, _const_spec((1, d)), _const_spec((d, d)), _const_spec((3 * d, d))],
        out_specs=[row_blk, tb_blk, t_blk, t_blk, tb_blk],
        out_shape=[jax.ShapeDtypeStruct((m, d), BF16),
                   tb_shape,
                   jax.ShapeDtypeStruct((batch, d, seq), F32),
                   jax.ShapeDtypeStruct((batch, d, seq), F32),
                   tb_shape],
        compiler_params=_cparams("parallel", "parallel"),
        name="qkv_proj_prompt",
    )(x, g, wk_bf, wt_bf)


def _convin_kernel(x_ref, g_ref, w_ref, gb_ref, u_ref, *, d):
    xn = _rms(x_ref[...], g_ref[...]).astype(BF16)
    y = jnp.dot(xn, w_ref[...], preferred_element_type=F32)
    gb_ref[...] = y[:, :d]
    u_ref[...] = y[:, d:2 * d] * y[:, 2 * d:]


def _convin_proj(x, g, w_bf, *, tm, name):
    m, d = x.shape
    n = w_bf.shape[1]
    row = lambda i: (i, 0)
    return pl.pallas_call(
        functools.partial(_convin_kernel, d=d),
        grid=(m // tm,),
        in_specs=[pl.BlockSpec((tm, d), row), _const_spec((1, d)), _const_spec((d, n))],
        out_specs=[pl.BlockSpec((tm, d), row)] * 2,
        out_shape=[jax.ShapeDtypeStruct((m, d), F32)] * 2,
        compiler_params=_cparams("parallel"),
        name=name,
    )(x, g, w_bf)


def _post_body(a_bf, x, wo_ref, gpost_ref, gpre_ref, wup_ref, wdown_ref, gmpost_ref):
    o = jnp.dot(a_bf, wo_ref[...], preferred_element_type=F32)
    x1 = x + _rms(o, gpost_ref[...])
    hn = _rms(x1, gpre_ref[...]).astype(BF16)
    h = jnp.dot(hn, wup_ref[...], preferred_element_type=F32)
    h = jnp.square(jnp.maximum(h, 0.0)).astype(BF16)
    mlp = jnp.dot(h, wdown_ref[...], preferred_element_type=F32)
    return x1 + _rms(mlp, gmpost_ref[...])


def _post_attn_kernel(a_ref, x_ref, wo_ref, gpost_ref, gpre_ref, wup_ref, wdown_ref, gmpost_ref, o_ref):
    o_ref[...] = _post_body(a_ref[...], x_ref[...], wo_ref, gpost_ref, gpre_ref,
                            wup_ref, wdown_ref, gmpost_ref)


def _post_conv_prompt_kernel(gb_ref, u_ref, halo_ref, cw_ref, x_ref, wo_ref, gpost_ref, gpre_ref,
                             wup_ref, wdown_ref, gmpost_ref, o_ref, *, blocks_per_seq):
    u = u_ref[...]
    seq_start = (pl.program_id(0) % blocks_per_seq) == 0
    halo = jnp.where(seq_start, 0.0, halo_ref[...])
    r = lax.broadcasted_iota(jnp.int32, u.shape, 0)
    u1 = jnp.where(r == 0, halo[CONV_HALO - 1:CONV_HALO, :], pltpu.roll(u, 1, axis=0))
    u2 = pltpu.roll(u, 2, axis=0)
    u2 = jnp.where(r == 0, halo[CONV_HALO - 2:CONV_HALO - 1, :], u2)
    u2 = jnp.where(r == 1, halo[CONV_HALO - 1:CONV_HALO, :], u2)
    cw = cw_ref[...]
    y = cw[0:1, :] * u2 + cw[1:2, :] * u1 + cw[2:3, :] * u
    a = (gb_ref[...] * y).astype(BF16)
    o_ref[...] = _post_body(a, x_ref[...], wo_ref, gpost_ref, gpre_ref, wup_ref, wdown_ref, gmpost_ref)


def _post_conv_sample_kernel(gb_ref, u_ref, s0_ref, s1_ref, cw_ref, x_ref, wo_ref, gpost_ref, gpre_ref,
                             wup_ref, wdown_ref, gmpost_ref, o_ref):
    cw = cw_ref[...]
    y = cw[0:1, :] * s0_ref[...] + cw[1:2, :] * s1_ref[...] + cw[2:3, :] * u_ref[...]
    a = (gb_ref[...] * y).astype(BF16)
    o_ref[...] = _post_body(a, x_ref[...], wo_ref, gpost_ref, gpre_ref, wup_ref, wdown_ref, gmpost_ref)


def _post_weight_specs(d, ff):
    return [_const_spec((d, d)), _const_spec((1, d)), _const_spec((1, d)),
            _const_spec((d, ff)), _const_spec((ff, d)), _const_spec((1, d))]


def _post_attn(a, x, weights, *, tm, name):
    m, d = x.shape
    ff = weights[3].shape[1]
    row = lambda i: (i, 0)
    blk = pl.BlockSpec((tm, d), row)
    return pl.pallas_call(
        _post_attn_kernel,
        grid=(m // tm,),
        in_specs=[blk, blk] + _post_weight_specs(d, ff),
        out_specs=blk,
        out_shape=jax.ShapeDtypeStruct((m, d), F32),
        compiler_params=_cparams("parallel"),
        name=name,
    )(a, x, *weights)


def _post_conv_prompt(gb, u, cw, x, weights, *, tm, seq):
    m, d = x.shape
    ff = weights[3].shape[1]
    row = lambda i: (i, 0)
    blk = pl.BlockSpec((tm, d), row)
    halo_blocks = tm // CONV_HALO
    halo = pl.BlockSpec((CONV_HALO, d), lambda i: (jnp.maximum(i * halo_blocks - 1, 0), 0))
    return pl.pallas_call(
        functools.partial(_post_conv_prompt_kernel, blocks_per_seq=seq // tm),
        grid=(m // tm,),
        in_specs=[blk, blk, halo, _const_spec((3, d)), blk] + _post_weight_specs(d, ff),
        out_specs=blk,
        out_shape=jax.ShapeDtypeStruct((m, d), F32),
        compiler_params=_cparams("parallel"),
        name="post_conv_prompt",
    )(gb, u, u, cw, x, *weights)


def _post_conv_sample(gb, u, s0, s1, cw, x, weights):
    m, d = x.shape
    ff = weights[3].shape[1]
    blk = pl.BlockSpec((m, d), lambda i: (0, 0))
    return pl.pallas_call(
        _post_conv_sample_kernel,
        grid=(1,),
        in_specs=[blk, blk, blk, blk, _const_spec((3, d)), blk] + _post_weight_specs(d, ff),
        out_specs=blk,
        out_shape=jax.ShapeDtypeStruct((m, d), F32),
        compiler_params=_cparams("arbitrary"),
        name="post_conv_sample",
    )(gb, u, s0, s1, cw, x, *weights)


def _softplus2(z2):
    return jnp.maximum(z2, 0.0) + jnp.log(1.0 + jnp.exp2(-jnp.abs(z2))) * LOG2E


def _sweep_unroll(nq):
    n_visits = nq * (nq + 1) // 2
    return max(u for u in range(1, ATT_UNROLL + 1) if nq % u == 0 and n_visits % u == 0)


def _sweep_trips(nq, unroll):
    diag_trips = nq // unroll
    n_trips = (nq * (nq + 1) // 2) // unroll
    return [(0, diag_trips, True), (diag_trips, n_trips - diag_trips, False)]


def _sweep_slots(nq, unroll):
    slots = 0
    for first, count, _ in _sweep_trips(nq, unroll):
        head = 1 if count and first % 2 else 0
        slots += head + (count - head) // 2 + (count - head) % 2
    return slots


def _sweep(qi_ref, kj_ref, bias_ref, k_ref, qt_ref, vt_ref, tri_ref, o_ref,
           acc_ref, carry_ref, z_even_ref, z_odd_ref, *, head_dim, unroll, slot_hook):
    hp = pl.program_id(1)
    blk = ATT_BLOCK
    nq = acc_ref.shape[0]
    n_trips = (nq * (nq + 1) // 2) // unroll
    biases = [bias_ref[hp * HEADS_PER_GROUP + h] * LOG2E for h in range(HEADS_PER_GROUP)]
    z_refs = (z_even_ref, z_odd_ref)

    def visits_of(trip):
        return [(qi_ref[trip * unroll + u], kj_ref[trip * unroll + u], h)
                for u in range(unroll) for h in range(HEADS_PER_GROUP)]

    def scores(trip, z_ref):
        for c, (qi, kj, h) in enumerate(visits_of(trip)):
            qt = qt_ref[qi]
            feat = lax.broadcasted_iota(jnp.int32, qt.shape, 0)
            own = (feat < head_dim) if h == 0 else (feat >= head_dim)
            qt_h = jnp.where(own, qt, jnp.zeros_like(qt))
            kb = k_ref[pl.ds(pl.multiple_of(kj * blk, blk), blk), :]
            z_ref[c] = jnp.dot(kb, qt_h, preferred_element_type=F32) + biases[h]

    def finish(trip, z_ref, diagonal):
        visits = visits_of(trip)
        if diagonal:
            key_idx = lax.broadcasted_iota(jnp.int32, (blk, blk), 0)
            qry_idx = lax.broadcasted_iota(jnp.int32, (blk, blk), 1)
            causal = key_idx < qry_idx
        sufs = []
        for c in range(len(visits)):
            sp = _softplus2(z_ref[c])
            if diagonal:
                sp = jnp.where(causal, sp, 0.0)
            sufs.append(jnp.dot(tri_ref[...], sp.astype(BF16), preferred_element_type=F32))
        yield
        outs = []
        for c, (qi, kj, h) in enumerate(visits):
            w = jnp.exp2(z_ref[c] - sufs[c])
            if diagonal:
                w = jnp.where(causal, w, 0.0)
            vt = vt_ref[kj, h * head_dim:(h + 1) * head_dim, :]
            outs.append(jnp.dot(vt, w.astype(BF16), preferred_element_type=F32))
        yield
        for (qi, kj, h), ob, suf in zip(visits, outs, sufs):
            tot = suf[0:1, :]
            if diagonal:
                acc_ref[qi, h] = ob
                carry_ref[qi, h] = tot
            else:
                carry = carry_ref[qi, h]
                acc_ref[qi, h] = acc_ref[qi, h] + ob * jnp.exp2(-carry)
                carry_ref[qi, h] = carry + tot

    def loop_body(slot, trips, diagonal):
        rider = slot_hook(slot)
        next(rider, None)
        for n, (trip, parity) in enumerate(trips):
            scores(jnp.minimum(trip + 1, n_trips - 1), z_refs[1 - parity])
            stages = finish(trip, z_refs[parity], diagonal)
            next(stages)
            if n == 0:
                next(rider, None)
            next(stages)
            if n == 0:
                next(rider, None)
            next(stages, None)
        next(rider, None)

    def run(first, count, diagonal, slot):
        if count and first % 2:
            loop_body(slot, [(first, 1)], diagonal)
            first, count, slot = first + 1, count - 1, slot + 1

        def pair(p, c):
            loop_body(slot + p, [(first + 2 * p, 0), (first + 2 * p + 1, 1)], diagonal)
            return c

        lax.fori_loop(0, count // 2, pair, 0)
        slot += count // 2
        if count % 2:
            loop_body(slot, [(first + count - 1, 0)], diagonal)
            slot += 1
        return slot

    scores(0, z_even_ref)
    slot = 0
    for first, count, diagonal in _sweep_trips(nq, unroll):
        slot = run(first, count, diagonal, slot)

    def out_body(qi, c):
        ot = jnp.concatenate([acc_ref[qi, h] for h in range(HEADS_PER_GROUP)], axis=0)
        o_ref[pl.ds(pl.multiple_of(qi * blk, blk), blk), :] = ot.T.astype(BF16)
        return c

    lax.fori_loop(0, nq, out_body, 0)


def _attn_kernel(qi_ref, kj_ref, *refs, head_dim, unroll):
    _sweep(qi_ref, kj_ref, *refs, head_dim=head_dim, unroll=unroll, slot_hook=lambda slot: iter(()))


def _decode_init(q_ref, knew_ref, vnew_ref, hmask, bias, acc_ref, onew_ref, carry_ref, *, past_len, dec_seq):
    n_heads = hmask.shape[0]
    q_pos = past_len + dec_seq - 1
    k_pos = past_len + lax.broadcasted_iota(jnp.int32, (n_heads, dec_seq), 1)
    visible = k_pos < q_pos
    z = jnp.sum(q_ref[...] * knew_ref[...] * hmask, axis=1, keepdims=True) + bias
    sp = jnp.where(visible, _softplus(z), 0.0)
    w = jnp.where(visible, jnp.exp(z - sp), 0.0)
    onew_ref[...] = jnp.sum(w * hmask, axis=0, keepdims=True) * vnew_ref[...]
    carry_ref[...] = sp
    acc_ref[...] = jnp.zeros_like(acc_ref)


def _decode_pages(k_pages, v_pages, q_ref, hmask, bias, tri, acc_ref, carry_ref, valid=None):
    n_heads = hmask.shape[0]
    n = len(k_pages)
    page = tri.shape[0]
    flat = lambda ref: ref[...].reshape(-1, page).astype(BF16)
    q_bd = (q_ref[...] * hmask).astype(BF16)
    kcat = jnp.concatenate([flat(kp) for kp in k_pages], axis=1)
    z = jnp.dot(q_bd, kcat, preferred_element_type=F32) + bias
    yield
    zs = [z[:, p * page:(p + 1) * page] for p in range(n)]
    sp = jnp.concatenate([_softplus(zp) for zp in zs], axis=0)
    if valid is not None:
        sp = jnp.where(valid, sp, 0.0)
    sp_hi = sp.astype(BF16)
    sp_lo = (sp - sp_hi.astype(F32)).astype(BF16)
    suf = (jnp.dot(sp_hi, tri, preferred_element_type=F32)
           + jnp.dot(sp_lo, tri, preferred_element_type=F32))
    yield
    carry = carry_ref[...]
    ws = [None] * n
    for p in reversed(range(n)):
        suf_p = suf[p * n_heads:(p + 1) * n_heads, :]
        w = jnp.exp(zs[p] - suf_p - carry)
        if valid is not None:
            w = jnp.where(valid, w, 0.0)
        ws[p] = w.astype(BF16)
        carry = carry + suf_p[:, 0:1]
    w_all = jnp.concatenate(ws, axis=1)
    w_pad = jnp.concatenate([w_all, jnp.zeros((LANES - n_heads, n * page), BF16)], axis=0)
    vcat = jnp.concatenate([flat(vp) for vp in v_pages], axis=1)
    pv = lax.dot_general(vcat, w_pad, (((1,), (1,)), ((), ())),
                         preferred_element_type=F32)
    yield
    acc_ref[...] += pv
    carry_ref[...] = carry


def _decode_output(acc_ref, onew_ref, hmask):
    n_heads = hmask.shape[0]
    acc_t = acc_ref[...].T
    return jnp.sum(acc_t[:n_heads] * hmask, axis=0, keepdims=True) + onew_ref[...]


def _decode_kernel(pt_ref, q_ref, knew_ref, vnew_ref, hmask_ref, bias_ref, tri_ref, *refs,
                   n_pages, page_size, dec_seq):
    del pt_ref
    pps = PAGES_PER_STEP
    k_refs = refs[:pps]
    v_refs = refs[pps:2 * pps]
    o_ref = refs[2 * pps]
    acc_ref, onew_ref, carry_ref = refs[2 * pps + 1:]
    step = pl.program_id(1)
    bias = bias_ref[...]
    hmask = hmask_ref[...]

    @pl.when(step == 0)
    def _():
        _decode_init(q_ref, knew_ref, vnew_ref, hmask, bias, acc_ref, onew_ref, carry_ref,
                     past_len=n_pages * page_size, dec_seq=dec_seq)

    for _ in _decode_pages(k_refs, v_refs, q_ref, hmask, bias, tri_ref[...], acc_ref, carry_ref):
        pass

    @pl.when(step == pl.num_programs(1) - 1)
    def _():
        o_ref[...] = _decode_output(acc_ref, onew_ref, hmask)


def _decode_operands(bias, d, page_size):
    n_heads = bias.shape[0]
    head_dim = d // n_heads
    hmask = (lax.broadcasted_iota(jnp.int32, (n_heads, d), 1) // head_dim
             == lax.broadcasted_iota(jnp.int32, (n_heads, d), 0)).astype(F32)
    tri = (lax.broadcasted_iota(jnp.int32, (page_size, page_size), 0)
           >= lax.broadcasted_iota(jnp.int32, (page_size, page_size), 1)).astype(BF16)
    return hmask, bias.reshape(n_heads, 1), tri


def _decode_scratch(n_heads, d):
    return [pltpu.VMEM((d, LANES), F32), pltpu.VMEM((1, d), F32), pltpu.VMEM((n_heads, 1), F32)]


def _decode_attention(q, k_new, v_new, cache_kt, cache_vt, layer, page_table, bias):
    db, _, d = q.shape
    n_heads, head_dim, page_size = cache_kt.shape[2:]
    n_pages = page_table.shape[1]
    pps = PAGES_PER_STEP
    steps = n_pages // pps
    hmask, bias_col, tri = _decode_operands(bias, d, page_size)

    def page_spec(p):
        return pl.BlockSpec((None, None, n_heads, head_dim, page_size),
                            lambda b, s, pt: (layer, pt[b, (steps - 1 - s) * pps + p], 0, 0, 0))

    per_seq = pl.BlockSpec((None, 1, d), lambda b, s, pt: (b, 0, 0))
    grid_spec = pltpu.PrefetchScalarGridSpec(
        num_scalar_prefetch=1,
        grid=(db, steps),
        in_specs=[
            per_seq, per_seq, per_seq,
            pl.BlockSpec((n_heads, d), lambda b, s, pt: (0, 0)),
            pl.BlockSpec((n_heads, 1), lambda b, s, pt: (0, 0)),
            pl.BlockSpec((page_size, page_size), lambda b, s, pt: (0, 0)),
        ] + [page_spec(p) for p in range(pps)] * 2,
        out_specs=per_seq,
        scratch_shapes=_decode_scratch(n_heads, d),
    )
    return pl.pallas_call(
        functools.partial(_decode_kernel, n_pages=n_pages, page_size=page_size, dec_seq=1),
        grid_spec=grid_spec,
        out_shape=jax.ShapeDtypeStruct((db, 1, d), F32),
        compiler_params=_cparams("parallel", "arbitrary"),
        name="decode_attention",
    )(page_table, q, k_new, v_new, hmask, bias_col, tri,
      *([cache_kt] * pps), *([cache_vt] * pps))


def _attn_decode_kernel(qi_ref, kj_ref, pt_ref, bias_ref, k_ref, qt_ref, vt_ref, tri_ref,
                        q_ref, knew_ref, vnew_ref, hmask_ref, dbias_ref, dtri_ref, ck_hbm, cv_hbm,
                        o_ref, od_ref,
                        acc_ref, carry_ref, z_even_ref, z_odd_ref,
                        dacc_ref, donew_ref, dcarry_ref, kbuf_ref, vbuf_ref, sem_ref,
                        *, head_dim, unroll, layer, n_slots, dec_seq):
    seq_id = pl.program_id(0) * pl.num_programs(1) + pl.program_id(1)
    n_pages = pt_ref.shape[1]
    chunk_pages = kbuf_ref.shape[1]
    page_size = kbuf_ref.shape[-1]
    n_chunks = n_pages // chunk_pages
    dbias = dbias_ref[...]
    hmask = hmask_ref[...]

    def chunk_copies(chunk, ring):
        first_page = n_pages - chunk_pages * (jnp.minimum(chunk, n_chunks - 1) + 1)
        copies = []
        for j in range(chunk_pages):
            page = pt_ref[seq_id, first_page + j]
            copies.append(pltpu.make_async_copy(ck_hbm.at[layer, page], kbuf_ref.at[ring, j], sem_ref.at[ring]))
            copies.append(pltpu.make_async_copy(cv_hbm.at[layer, page], vbuf_ref.at[ring, j], sem_ref.at[ring]))
        return copies

    def slot_hook(slot):
        ring = slot % 2
        for cp in chunk_copies(slot, ring):
            cp.wait()
        for cp in chunk_copies(slot + 1, 1 - ring):
            cp.start()
        yield from _decode_pages([kbuf_ref.at[ring, j] for j in range(chunk_pages)],
                                 [vbuf_ref.at[ring, j] for j in range(chunk_pages)],
                                 q_ref, hmask, dbias, dtri_ref[...], dacc_ref, dcarry_ref,
                                 valid=slot < n_chunks)

    _decode_init(q_ref, knew_ref, vnew_ref, hmask, dbias, dacc_ref, donew_ref, dcarry_ref,
                 past_len=n_pages * page_size, dec_seq=dec_seq)
    for cp in chunk_copies(0, 0):
        cp.start()
    _sweep(qi_ref, kj_ref, bias_ref, k_ref, qt_ref, vt_ref, tri_ref, o_ref,
           acc_ref, carry_ref, z_even_ref, z_odd_ref, head_dim=head_dim, unroll=unroll, slot_hook=slot_hook)
    for cp in chunk_copies(n_slots, n_slots % 2):
        cp.wait()
    od_ref[...] = _decode_output(dacc_ref, donew_ref, hmask)


def _prompt_attention(k_bf, qt_bf, vt_bf, bias, tri, *, batch, seq, head_dim, decode=None):
    m, d = k_bf.shape
    blk = ATT_BLOCK
    nq = seq // blk
    assert nq >= 2
    groups = d // LANES
    pairs = [(qi, qi) for qi in range(nq)]
    pairs += [(qi, kj) for qi in range(1, nq) for kj in range(qi - 1, -1, -1)]
    unroll = _sweep_unroll(nq)
    qi_tab = jnp.asarray([p[0] for p in pairs], jnp.int32)
    kj_tab = jnp.asarray([p[1] for p in pairs], jnp.int32)
    n_prefetch = 2 if decode is None else 3
    blocked = pl.BlockSpec((nq, LANES, blk), lambda b, g, *_: (b, g, 0))
    seq_cols = pl.BlockSpec((seq, LANES), lambda b, g, *_: (b, g))
    z_tiles = pltpu.VMEM((unroll * HEADS_PER_GROUP, blk, blk), F32)
    in_specs = [pl.BlockSpec(memory_space=pltpu.SMEM), seq_cols, blocked, blocked,
                pl.BlockSpec((blk, blk), lambda b, g, *_: (0, 0))]
    scratch = [pltpu.VMEM((nq, HEADS_PER_GROUP, head_dim, blk), F32),
               pltpu.VMEM((nq, HEADS_PER_GROUP, 1, blk), F32), z_tiles, z_tiles]
    out_sweep = jax.ShapeDtypeStruct((m, d), BF16)
    if decode is None:
        return pl.pallas_call(
            functools.partial(_attn_kernel, head_dim=head_dim, unroll=unroll),
            grid_spec=pltpu.PrefetchScalarGridSpec(
                num_scalar_prefetch=n_prefetch, grid=(batch, groups),
                in_specs=in_specs, out_specs=seq_cols, scratch_shapes=scratch),
            out_shape=out_sweep,
            compiler_params=_cparams("parallel", "parallel"),
            name="prompt_attention",
        )(qi_tab, kj_tab, bias, k_bf, qt_bf, vt_bf, tri)

    q, k_new, v_new, cache_kt, cache_vt, layer, page_table, dec_bias = decode
    db = q.shape[0]
    assert db == batch * groups
    n_heads, _, page_size = cache_kt.shape[2:]
    n_pages = page_table.shape[1]
    n_slots = _sweep_slots(nq, unroll)
    chunk_pages = _decode_chunk_pages(n_pages, n_slots)
    hmask, bias_col, dtri = _decode_operands(dec_bias, d, page_size)
    per_seq = pl.BlockSpec((None, 1, d), lambda b, g, *_: (b * groups + g, 0, 0))
    in_specs += [
        per_seq, per_seq, per_seq,
        pl.BlockSpec((n_heads, d), lambda b, g, *_: (0, 0)),
        pl.BlockSpec((n_heads, 1), lambda b, g, *_: (0, 0)),
        pl.BlockSpec((page_size, page_size), lambda b, g, *_: (0, 0)),
        pl.BlockSpec(memory_space=pl.ANY), pl.BlockSpec(memory_space=pl.ANY),
    ]
    ring = pltpu.VMEM((2, chunk_pages, n_heads, head_dim, page_size), F32)
    scratch += _decode_scratch(n_heads, d) + [ring, ring, pltpu.SemaphoreType.DMA((2,))]
    return pl.pallas_call(
        functools.partial(_attn_decode_kernel, head_dim=head_dim, unroll=unroll, layer=layer,
                          n_slots=n_slots, dec_seq=1),
        grid_spec=pltpu.PrefetchScalarGridSpec(
            num_scalar_prefetch=n_prefetch, grid=(batch, groups),
            in_specs=in_specs, out_specs=[seq_cols, per_seq], scratch_shapes=scratch),
        out_shape=[out_sweep, jax.ShapeDtypeStruct((db, 1, d), F32)],
        compiler_params=_cparams("parallel", "parallel"),
        name="prompt_attention_with_decode",
    )(qi_tab, kj_tab, page_table, bias, k_bf, qt_bf, vt_bf, tri,
      q, k_new, v_new, hmask, bias_col, dtri, cache_kt, cache_vt)


def _decode_chunk_pages(n_pages, n_slots):
    fits = [c for c in range(1, n_pages + 1) if n_pages % c == 0 and n_pages // c <= n_slots]
    return fits[0] if fits else None


def kernel(x_prompt, x_sample, cache_k, cache_v, state_conv, page_table, norm_mix_pre, norm_mix_post,
           norm_mlp_pre, norm_mlp_post, w_qkv, sb_bias, w_attn_out, w_conv_in, conv_w, w_conv_out,
           w_mlp_up, w_mlp_down):
    batch, seq, d = x_prompt.shape
    db, ts, _ = x_sample.shape
    assert ts == 1, "the sample group decodes one token per sequence"
    depth = norm_mix_pre.shape[0]
    n_heads = sb_bias.shape[1]
    head_dim = d // n_heads
    assert head_dim * HEADS_PER_GROUP == LANES
    assert seq % ATT_BLOCK == 0 and page_table.shape[1] % PAGES_PER_STEP == 0
    scale = head_dim ** -0.5
    tm = 512 if seq % 512 == 0 else ATT_BLOCK
    tm_post = 256
    nq = seq // ATT_BLOCK
    fuse_decode = (db == batch * (d // LANES) and
                   _decode_chunk_pages(page_table.shape[1], _sweep_slots(nq, _sweep_unroll(nq))) is not None)

    xp = x_prompt.reshape(batch * seq, d)
    xs = x_sample.reshape(db * ts, d)
    tri = (lax.broadcasted_iota(jnp.int32, (ATT_BLOCK, ATT_BLOCK), 1)
           >= lax.broadcasted_iota(jnp.int32, (ATT_BLOCK, ATT_BLOCK), 0)).astype(BF16)
    cache_kt = jnp.transpose(cache_k, (0, 1, 3, 4, 2))
    cache_vt = jnp.transpose(cache_v, (0, 1, 3, 4, 2))

    row = lambda g, i: g[i].reshape(1, d)
    heads_major = lambda t: jnp.transpose(t.reshape(batch, n_heads, head_dim, seq), (0, 3, 1, 2))
    kp_new, vp_new, cp_new, ks_new, vs_new, cs_new = [], [], [], [], [], []
    for i in range(depth):
        post_w = [None, row(norm_mix_post, i), row(norm_mlp_pre, i), w_mlp_up[i].astype(BF16),
                  w_mlp_down[i].astype(BF16), row(norm_mlp_post, i)]
        g_pre = row(norm_mix_pre, i)
        if i % 2 == 0:
            a = i // 2
            w_bf = w_qkv[a].astype(BF16)
            wt_bf = w_qkv[a].T.astype(BF16)
            post_w[0] = w_attn_out[a].astype(BF16)
            k_bf, qt_bf, kt, vt, vt_bf = _qkv_proj_prompt(xp, g_pre, w_bf[:, d:2 * d], wt_bf, scale * LOG2E,
                                                          batch=batch, seq=seq, tm=tm)
            qs_bf, ks, vs = _qkv_proj_sample(xs, g_pre, w_bf, scale)
            per_seq = lambda t: t.reshape(db, 1, d)
            decode = (per_seq(qs_bf.astype(F32)), per_seq(ks), per_seq(vs), cache_kt, cache_vt,
                      a, page_table, sb_bias[a])
            if fuse_decode:
                o_bf, os_ = _prompt_attention(k_bf, qt_bf, vt_bf, sb_bias[a], tri,
                                              batch=batch, seq=seq, head_dim=head_dim, decode=decode)
            else:
                o_bf = _prompt_attention(k_bf, qt_bf, vt_bf, sb_bias[a], tri,
                                         batch=batch, seq=seq, head_dim=head_dim)
                os_ = _decode_attention(*decode)
            xp = _post_attn(o_bf, xp, post_w, tm=tm_post, name="post_attn_prompt")
            xs = _post_attn(os_.reshape(db, d).astype(BF16), xs, post_w, tm=db * ts, name="post_attn_sample")
            kp_new.append(heads_major(kt))
            vp_new.append(heads_major(vt))
            ks_new.append(ks.reshape(db, ts, n_heads, head_dim))
            vs_new.append(vs.reshape(db, ts, n_heads, head_dim))
        else:
            c = i // 2
            w_bf = w_conv_in[c].astype(BF16)
            post_w[0] = w_conv_out[c].astype(BF16)
            gb, u = _convin_proj(xp, g_pre, w_bf, tm=tm, name="convin_prompt")
            xp = _post_conv_prompt(gb, u, conv_w[c], xp, post_w, tm=tm_post, seq=seq)
            cp_new.append(u.reshape(batch, seq, d)[:, seq - 2:, :])
            gbs, us = _convin_proj(xs, g_pre, w_bf, tm=db * ts, name="convin_sample")
            st = state_conv[c]
            xs = _post_conv_sample(gbs, us, st[:, 0, :], st[:, 1, :], conv_w[c], xs, post_w)
            cs_new.append(jnp.concatenate([st[:, 1:, :], us.reshape(db, ts, d)], axis=1))
    return (xp.reshape(batch, seq, d), xs.reshape(db, ts, d),
            jnp.stack(kp_new), jnp.stack(vp_new), jnp.stack(cp_new),
            jnp.stack(ks_new), jnp.stack(vs_new), jnp.stack(cs_new))
```

```python
import functools

import jax
import jax.numpy as jnp
from jax import lax
from jax.experimental import pallas as pl
from jax.experimental.pallas import tpu as pltpu

F32 = jnp.float32
BF16 = jnp.bfloat16

RMS_EPS = 1e-6
VMEM_LIMIT_BYTES = 56 * 1024 * 1024
LANES = 128
HEADS_PER_GROUP = 2
ATT_BLOCK = 256
ATT_UNROLL = 4
LOG2E = 1.4426950408889634
PAGES_PER_STEP = 8
CONV_HALO = 8


def _cparams(*sem):
    return pltpu.CompilerParams(dimension_semantics=sem, vmem_limit_bytes=VMEM_LIMIT_BYTES)


def _const_spec(shape):
    nd = len(shape)
    return pl.BlockSpec(shape, lambda *_: (0,) * nd, pipeline_mode=pl.Buffered(1))


def _rms(x, g):
    inv = lax.rsqrt(jnp.mean(x * x, axis=-1, keepdims=True) + RMS_EPS)
    return (x * inv) * g


def _softplus(z):
    return jnp.maximum(z, 0.0) + jnp.log(1.0 + jnp.exp(-jnp.abs(z)))


def _qkv_kernel(x_ref, g_ref, w_ref, q_ref, k_ref, v_ref, *, d, scale):
    xn = _rms(x_ref[...], g_ref[...]).astype(BF16)
    qkv = jnp.dot(xn, w_ref[...], preferred_element_type=F32)
    q_ref[...] = (qkv[:, :d] * scale).astype(BF16)
    k_ref[...] = qkv[:, d:2 * d]
    v_ref[...] = qkv[:, 2 * d:]


def _qkv_prompt_kernel(x_ref, g_ref, wk_ref, wt_ref, *refs, d, scale, n_prev):
    prev_refs, (kb_ref, qtb_ref, kt_ref, vt_ref, vtb_ref) = refs[:len(refs) - 5], refs[len(refs) - 5:]
    xn = _rms(x_ref[...], g_ref[...]).astype(BF16)
    kb_ref[...] = jnp.dot(xn, wk_ref[...], preferred_element_type=F32).astype(BF16)
    t = lax.dot_general(wt_ref[...], xn, (((1,), (1,)), ((), ())), preferred_element_type=F32)
    if n_prev:
        kt_prev_ref, vt_prev_ref = prev_refs
        kt_ref[:n_prev] = kt_prev_ref[...]
        vt_ref[:n_prev] = vt_prev_ref[...]
    kt_ref[n_prev] = t[d:2 * d]
    vt_ref[n_prev] = t[2 * d:]
    qtb = (t[:d] * scale).astype(BF16)
    vtb = t[2 * d:].astype(BF16)
    for c in range(vtb_ref.shape[0]):
        qtb_ref[c] = qtb[:, c * ATT_BLOCK:(c + 1) * ATT_BLOCK]
        vtb_ref[c] = vtb[:, c * ATT_BLOCK:(c + 1) * ATT_BLOCK]


def _qkv_proj_sample(x, g, w_stack, layer, scale):
    m, d = x.shape
    blk = pl.BlockSpec((m, d), lambda i: (0, 0))
    return pl.pallas_call(
        functools.partial(_qkv_kernel, d=d, scale=scale),
        grid=(1,),
        in_specs=[blk, _const_spec((1, d)), _param_spec(w_stack, layer)],
        out_specs=[blk, blk, blk],
        out_shape=[jax.ShapeDtypeStruct((m, d), BF16),
                   jax.ShapeDtypeStruct((m, d), F32),
                   jax.ShapeDtypeStruct((m, d), F32)],
        compiler_params=_cparams("arbitrary"),
        name="qkv_proj_sample",
    )(x, g, w_stack)


def _qkv_proj_prompt(x, g, w_stack, wt_stack, layer, scale, prev, *, batch, seq, tm):
    m, d = x.shape
    nb = seq // tm
    cb = tm // ATT_BLOCK
    n_prev = prev[0].shape[0] if prev else 0
    row_blk = pl.BlockSpec((tm, d), lambda b, i: (b * nb + i, 0))
    stack_blk = lambda n: pl.BlockSpec((n, None, d, tm), lambda b, i: (0, b, 0, i))
    tb_blk = pl.BlockSpec((cb, d, ATT_BLOCK), lambda b, i: (b * nb + i, 0, 0))
    tb_shape = jax.ShapeDtypeStruct((m // ATT_BLOCK, d, ATT_BLOCK), BF16)
    stack_shape = jax.ShapeDtypeStruct((n_prev + 1, batch, d, seq), F32)
    wk_spec = pl.BlockSpec((None, d, d), lambda *_: (layer, 0, 1), pipeline_mode=pl.Buffered(1))
    return pl.pallas_call(
        functools.partial(_qkv_prompt_kernel, d=d, scale=scale, n_prev=n_prev),
        grid=(batch, nb),
        in_specs=[row_blk, _const_spec((1, d)), wk_spec, _param_spec(wt_stack, layer)]
                 + [stack_blk(n_prev)] * len(prev),
        out_specs=[row_blk, tb_blk, stack_blk(n_prev + 1), stack_blk(n_prev + 1), tb_blk],
        out_shape=[jax.ShapeDtypeStruct((m, d), BF16), tb_shape, stack_shape, stack_shape, tb_shape],
        compiler_params=_cparams("parallel", "parallel"),
        name="qkv_proj_prompt",
    )(x, g, w_stack, wt_stack, *prev)


def _convin_kernel(x_ref, g_ref, w_ref, gb_ref, u_ref, *, d):
    xn = _rms(x_ref[...], g_ref[...]).astype(BF16)
    y = jnp.dot(xn, w_ref[...], preferred_element_type=F32)
    gb_ref[...] = y[:, :d]
    u_ref[...] = y[:, d:2 * d] * y[:, 2 * d:]


def _convin_proj(x, g, w_stack, layer, *, tm, name):
    m, d = x.shape
    row = lambda i: (i, 0)
    return pl.pallas_call(
        functools.partial(_convin_kernel, d=d),
        grid=(m // tm,),
        in_specs=[pl.BlockSpec((tm, d), row), _const_spec((1, d)), _param_spec(w_stack, layer)],
        out_specs=[pl.BlockSpec((tm, d), row)] * 2,
        out_shape=[jax.ShapeDtypeStruct((m, d), F32)] * 2,
        compiler_params=_cparams("parallel"),
        name=name,
    )(x, g, w_stack)


def _post_body(a_bf, x, wo_ref, gpost_ref, gpre_ref, wup_ref, wdown_ref, gmpost_ref):
    o = jnp.dot(a_bf, wo_ref[...], preferred_element_type=F32)
    x1 = x + _rms(o, gpost_ref[...])
    hn = _rms(x1, gpre_ref[...]).astype(BF16)
    h = jnp.dot(hn, wup_ref[...], preferred_element_type=F32)
    h = jnp.square(jnp.maximum(h, 0.0)).astype(BF16)
    mlp = jnp.dot(h, wdown_ref[...], preferred_element_type=F32)
    return x1 + _rms(mlp, gmpost_ref[...])


def _post_attn_kernel(a_ref, x_ref, wo_ref, gpost_ref, gpre_ref, wup_ref, wdown_ref, gmpost_ref, o_ref):
    o_ref[...] = _post_body(a_ref[...], x_ref[...], wo_ref, gpost_ref, gpre_ref,
                            wup_ref, wdown_ref, gmpost_ref)


def _post_conv_prompt_kernel(gb_ref, u_ref, halo_ref, cw_ref, x_ref, wo_ref, gpost_ref, gpre_ref,
                             wup_ref, wdown_ref, gmpost_ref, o_ref, *, blocks_per_seq):
    u = u_ref[...]
    seq_start = (pl.program_id(0) % blocks_per_seq) == 0
    halo = jnp.where(seq_start, 0.0, halo_ref[...])
    r = lax.broadcasted_iota(jnp.int32, u.shape, 0)
    u1 = jnp.where(r == 0, halo[CONV_HALO - 1:CONV_HALO, :], pltpu.roll(u, 1, axis=0))
    u2 = pltpu.roll(u, 2, axis=0)
    u2 = jnp.where(r == 0, halo[CONV_HALO - 2:CONV_HALO - 1, :], u2)
    u2 = jnp.where(r == 1, halo[CONV_HALO - 1:CONV_HALO, :], u2)
    cw = cw_ref[...]
    y = cw[0:1, :] * u2 + cw[1:2, :] * u1 + cw[2:3, :] * u
    a = (gb_ref[...] * y).astype(BF16)
    o_ref[...] = _post_body(a, x_ref[...], wo_ref, gpost_ref, gpre_ref, wup_ref, wdown_ref, gmpost_ref)


def _post_conv_sample_kernel(gb_ref, u_ref, s0_ref, s1_ref, cw_ref, x_ref, wo_ref, gpost_ref, gpre_ref,
                             wup_ref, wdown_ref, gmpost_ref, o_ref):
    cw = cw_ref[...]
    y = cw[0:1, :] * s0_ref[...] + cw[1:2, :] * s1_ref[...] + cw[2:3, :] * u_ref[...]
    a = (gb_ref[...] * y).astype(BF16)
    o_ref[...] = _post_body(a, x_ref[...], wo_ref, gpost_ref, gpre_ref, wup_ref, wdown_ref, gmpost_ref)


def _param_spec(arr, layer):
    if layer is None:
        return _const_spec(arr.shape)
    tail = (0,) * (arr.ndim - 1)
    return pl.BlockSpec((None,) + arr.shape[1:], lambda *_: (layer,) + tail, pipeline_mode=pl.Buffered(1))


def _post_weight_specs(weights):
    return [_param_spec(arr, layer) for arr, layer in weights]


def _post_attn(a, x, weights, *, tm, name):
    m, d = x.shape
    row = lambda i: (i, 0)
    blk = pl.BlockSpec((tm, d), row)
    return pl.pallas_call(
        _post_attn_kernel,
        grid=(m // tm,),
        in_specs=[blk, blk] + _post_weight_specs(weights),
        out_specs=blk,
        out_shape=jax.ShapeDtypeStruct((m, d), F32),
        compiler_params=_cparams("parallel"),
        name=name,
    )(a, x, *[arr for arr, _ in weights])


def _post_conv_prompt(gb, u, cw, x, weights, *, tm, seq):
    m, d = x.shape
    row = lambda i: (i, 0)
    blk = pl.BlockSpec((tm, d), row)
    halo_blocks = tm // CONV_HALO
    halo = pl.BlockSpec((CONV_HALO, d), lambda i: (jnp.maximum(i * halo_blocks - 1, 0), 0))
    return pl.pallas_call(
        functools.partial(_post_conv_prompt_kernel, blocks_per_seq=seq // tm),
        grid=(m // tm,),
        in_specs=[blk, blk, halo, _const_spec((3, d)), blk] + _post_weight_specs(weights),
        out_specs=blk,
        out_shape=jax.ShapeDtypeStruct((m, d), F32),
        compiler_params=_cparams("parallel"),
        name="post_conv_prompt",
    )(gb, u, u, cw, x, *[arr for arr, _ in weights])


def _post_conv_sample(gb, u, s0, s1, cw, x, weights):
    m, d = x.shape
    blk = pl.BlockSpec((m, d), lambda i: (0, 0))
    return pl.pallas_call(
        _post_conv_sample_kernel,
        grid=(1,),
        in_specs=[blk, blk, blk, blk, _const_spec((3, d)), blk] + _post_weight_specs(weights),
        out_specs=blk,
        out_shape=jax.ShapeDtypeStruct((m, d), F32),
        compiler_params=_cparams("arbitrary"),
        name="post_conv_sample",
    )(gb, u, s0, s1, cw, x, *[arr for arr, _ in weights])


def _softplus2(z2):
    return jnp.maximum(z2, 0.0) + jnp.log(1.0 + jnp.exp2(-jnp.abs(z2))) * LOG2E


def _sweep_unroll(nq):
    n_visits = nq * (nq + 1) // 2
    return max(u for u in range(1, ATT_UNROLL + 1) if nq % u == 0 and n_visits % u == 0)


def _sweep_trips(nq, unroll):
    diag_trips = nq // unroll
    n_trips = (nq * (nq + 1) // 2) // unroll
    return [(0, diag_trips, True), (diag_trips, n_trips - diag_trips, False)]


def _sweep_slots(nq, unroll):
    slots = 0
    for first, count, _ in _sweep_trips(nq, unroll):
        head = 1 if count and first % 2 else 0
        slots += head + (count - head) // 2 + (count - head) % 2
    return slots


def _sweep(qi_ref, kj_ref, bias_ref, k_ref, qt_ref, vt_ref, tri_ref, o_ref,
           acc_ref, carry_ref, z_even_ref, z_odd_ref, *, head_dim, unroll, slot_hook):
    hp = pl.program_id(1)
    blk = ATT_BLOCK
    nq = acc_ref.shape[0]
    n_trips = (nq * (nq + 1) // 2) // unroll
    biases = [bias_ref[hp * HEADS_PER_GROUP + h] * LOG2E for h in range(HEADS_PER_GROUP)]
    z_refs = (z_even_ref, z_odd_ref)

    def visits_of(trip):
        return [(qi_ref[trip * unroll + u], kj_ref[trip * unroll + u], h)
                for u in range(unroll) for h in range(HEADS_PER_GROUP)]

    def scores(trip, z_ref):
        for c, (qi, kj, h) in enumerate(visits_of(trip)):
            qt = qt_ref[qi]
            feat = lax.broadcasted_iota(jnp.int32, qt.shape, 0)
            own = (feat < head_dim) if h == 0 else (feat >= head_dim)
            qt_h = jnp.where(own, qt, jnp.zeros_like(qt))
            kb = k_ref[pl.ds(pl.multiple_of(kj * blk, blk), blk), :]
            z_ref[c] = jnp.dot(kb, qt_h, preferred_element_type=F32) + biases[h]

    def finish(trip, z_ref, diagonal):
        visits = visits_of(trip)
        if diagonal:
            key_idx = lax.broadcasted_iota(jnp.int32, (blk, blk), 0)
            qry_idx = lax.broadcasted_iota(jnp.int32, (blk, blk), 1)
            causal = key_idx < qry_idx
        sufs = []
        for c in range(len(visits)):
            sp = _softplus2(z_ref[c])
            if diagonal:
                sp = jnp.where(causal, sp, 0.0)
            sufs.append(jnp.dot(tri_ref[...], sp.astype(BF16), preferred_element_type=F32))
        yield
        outs = []
        for c, (qi, kj, h) in enumerate(visits):
            w = jnp.exp2(z_ref[c] - sufs[c])
            if diagonal:
                w = jnp.where(causal, w, 0.0)
            vt = vt_ref[kj, h * head_dim:(h + 1) * head_dim, :]
            outs.append(jnp.dot(vt, w.astype(BF16), preferred_element_type=F32))
        yield
        for (qi, kj, h), ob, suf in zip(visits, outs, sufs):
            tot = suf[0:1, :]
            if diagonal:
                acc_ref[qi, h] = ob
                carry_ref[qi, h] = tot
            else:
                carry = carry_ref[qi, h]
                acc_ref[qi, h] = acc_ref[qi, h] + ob * jnp.exp2(-carry)
                carry_ref[qi, h] = carry + tot

    def loop_body(slot, trips, diagonal):
        rider = slot_hook(slot)
        next(rider, None)
        for n, (trip, parity) in enumerate(trips):
            scores(jnp.minimum(trip + 1, n_trips - 1), z_refs[1 - parity])
            stages = finish(trip, z_refs[parity], diagonal)
            next(stages)
            if n == 0:
                next(rider, None)
            next(stages)
            if n == 0:
                next(rider, None)
            next(stages, None)
        next(rider, None)

    def run(first, count, diagonal, slot):
        if count and first % 2:
            loop_body(slot, [(first, 1)], diagonal)
            first, count, slot = first + 1, count - 1, slot + 1

        def pair(p, c):
            loop_body(slot + p, [(first + 2 * p, 0), (first + 2 * p + 1, 1)], diagonal)
            return c

        lax.fori_loop(0, count // 2, pair, 0)
        slot += count // 2
        if count % 2:
            loop_body(slot, [(first + count - 1, 0)], diagonal)
            slot += 1
        return slot

    scores(0, z_even_ref)
    slot = 0
    for first, count, diagonal in _sweep_trips(nq, unroll):
        slot = run(first, count, diagonal, slot)

    def out_body(qi, c):
        ot = jnp.concatenate([acc_ref[qi, h] for h in range(HEADS_PER_GROUP)], axis=0)
        o_ref[pl.ds(pl.multiple_of(qi * blk, blk), blk), :] = ot.T.astype(BF16)
        return c

    lax.fori_loop(0, nq, out_body, 0)


def _attn_kernel(qi_ref, kj_ref, *refs, head_dim, unroll):
    _sweep(qi_ref, kj_ref, *refs, head_dim=head_dim, unroll=unroll, slot_hook=lambda slot: iter(()))


def _decode_init(q_ref, knew_ref, vnew_ref, hmask, bias, acc_ref, onew_ref, carry_ref, *, past_len, dec_seq):
    n_heads = hmask.shape[0]
    q_pos = past_len + dec_seq - 1
    k_pos = past_len + lax.broadcasted_iota(jnp.int32, (n_heads, dec_seq), 1)
    visible = k_pos < q_pos
    z = jnp.sum(q_ref[...] * knew_ref[...] * hmask, axis=1, keepdims=True) + bias
    sp = jnp.where(visible, _softplus(z), 0.0)
    w = jnp.where(visible, jnp.exp(z - sp), 0.0)
    onew_ref[...] = jnp.sum(w * hmask, axis=0, keepdims=True) * vnew_ref[...]
    carry_ref[...] = sp
    acc_ref[...] = jnp.zeros_like(acc_ref)


def _decode_pages(k_pages, v_pages, q_ref, hmask, bias, tri, acc_ref, carry_ref, valid=None):
    n_heads = hmask.shape[0]
    n = len(k_pages)
    page = tri.shape[0]
    flat = lambda ref: ref[...].reshape(-1, page).astype(BF16)
    q_bd = (q_ref[...] * hmask).astype(BF16)
    kcat = jnp.concatenate([flat(kp) for kp in k_pages], axis=1)
    z = jnp.dot(q_bd, kcat, preferred_element_type=F32) + bias
    yield
    zs = [z[:, p * page:(p + 1) * page] for p in range(n)]
    sp = jnp.concatenate([_softplus(zp) for zp in zs], axis=0)
    if valid is not None:
        sp = jnp.where(valid, sp, 0.0)
    sp_hi = sp.astype(BF16)
    sp_lo = (sp - sp_hi.astype(F32)).astype(BF16)
    suf = (jnp.dot(sp_hi, tri, preferred_element_type=F32)
           + jnp.dot(sp_lo, tri, preferred_element_type=F32))
    yield
    carry = carry_ref[...]
    ws = [None] * n
    for p in reversed(range(n)):
        suf_p = suf[p * n_heads:(p + 1) * n_heads, :]
        w = jnp.exp(zs[p] - suf_p - carry)
        if valid is not None:
            w = jnp.where(valid, w, 0.0)
        ws[p] = w.astype(BF16)
        carry = carry + suf_p[:, 0:1]
    w_all = jnp.concatenate(ws, axis=1)
    w_pad = jnp.concatenate([w_all, jnp.zeros((LANES - n_heads, n * page), BF16)], axis=0)
    vcat = jnp.concatenate([flat(vp) for vp in v_pages], axis=1)
    pv = lax.dot_general(vcat, w_pad, (((1,), (1,)), ((), ())),
                         preferred_element_type=F32)
    yield
    acc_ref[...] += pv
    carry_ref[...] = carry


def _decode_output(acc_ref, onew_ref, hmask):
    n_heads = hmask.shape[0]
    acc_t = acc_ref[...].T
    return jnp.sum(acc_t[:n_heads] * hmask, axis=0, keepdims=True) + onew_ref[...]


def _decode_kernel(pt_ref, q_ref, knew_ref, vnew_ref, hmask_ref, bias_ref, tri_ref, *refs,
                   n_pages, page_size, dec_seq):
    del pt_ref
    pps = PAGES_PER_STEP
    k_refs = refs[:pps]
    v_refs = refs[pps:2 * pps]
    o_ref = refs[2 * pps]
    acc_ref, onew_ref, carry_ref = refs[2 * pps + 1:]
    step = pl.program_id(1)
    bias = bias_ref[...]
    hmask = hmask_ref[...]

    @pl.when(step == 0)
    def _():
        _decode_init(q_ref, knew_ref, vnew_ref, hmask, bias, acc_ref, onew_ref, carry_ref,
                     past_len=n_pages * page_size, dec_seq=dec_seq)

    for _ in _decode_pages(k_refs, v_refs, q_ref, hmask, bias, tri_ref[...], acc_ref, carry_ref):
        pass

    @pl.when(step == pl.num_programs(1) - 1)
    def _():
        o_ref[...] = _decode_output(acc_ref, onew_ref, hmask)


def _decode_operands(bias, d, page_size):
    n_heads = bias.shape[0]
    head_dim = d // n_heads
    hmask = (lax.broadcasted_iota(jnp.int32, (n_heads, d), 1) // head_dim
             == lax.broadcasted_iota(jnp.int32, (n_heads, d), 0)).astype(F32)
    tri = (lax.broadcasted_iota(jnp.int32, (page_size, page_size), 0)
           >= lax.broadcasted_iota(jnp.int32, (page_size, page_size), 1)).astype(BF16)
    return hmask, bias.reshape(n_heads, 1), tri


def _decode_scratch(n_heads, d):
    return [pltpu.VMEM((d, LANES), F32), pltpu.VMEM((1, d), F32), pltpu.VMEM((n_heads, 1), F32)]


def _decode_attention(q, k_new, v_new, cache_kt, cache_vt, layer, page_table, bias):
    db, _, d = q.shape
    n_heads, head_dim, page_size = cache_kt.shape[2:]
    n_pages = page_table.shape[1]
    pps = PAGES_PER_STEP
    steps = n_pages // pps
    hmask, bias_col, tri = _decode_operands(bias, d, page_size)

    def page_spec(p):
        return pl.BlockSpec((None, None, n_heads, head_dim, page_size),
                            lambda b, s, pt: (layer, pt[b, (steps - 1 - s) * pps + p], 0, 0, 0))

    per_seq = pl.BlockSpec((None, 1, d), lambda b, s, pt: (b, 0, 0))
    grid_spec = pltpu.PrefetchScalarGridSpec(
        num_scalar_prefetch=1,
        grid=(db, steps),
        in_specs=[
            per_seq, per_seq, per_seq,
            pl.BlockSpec((n_heads, d), lambda b, s, pt: (0, 0)),
            pl.BlockSpec((n_heads, 1), lambda b, s, pt: (0, 0)),
            pl.BlockSpec((page_size, page_size), lambda b, s, pt: (0, 0)),
        ] + [page_spec(p) for p in range(pps)] * 2,
        out_specs=per_seq,
        scratch_shapes=_decode_scratch(n_heads, d),
    )
    return pl.pallas_call(
        functools.partial(_decode_kernel, n_pages=n_pages, page_size=page_size, dec_seq=1),
        grid_spec=grid_spec,
        out_shape=jax.ShapeDtypeStruct((db, 1, d), F32),
        compiler_params=_cparams("parallel", "arbitrary"),
        name="decode_attention",
    )(page_table, q, k_new, v_new, hmask, bias_col, tri,
      *([cache_kt] * pps), *([cache_vt] * pps))


def _attn_decode_kernel(qi_ref, kj_ref, pt_ref, bias_ref, k_ref, qt_ref, vt_ref, tri_ref,
                        q_ref, knew_ref, vnew_ref, hmask_ref, dbias_ref, dtri_ref, ck_hbm, cv_hbm,
                        o_ref, od_ref,
                        acc_ref, carry_ref, z_even_ref, z_odd_ref,
                        dacc_ref, donew_ref, dcarry_ref, kbuf_ref, vbuf_ref, sem_ref,
                        *, head_dim, unroll, layer, n_slots, dec_seq):
    seq_id = pl.program_id(0) * pl.num_programs(1) + pl.program_id(1)
    n_pages = pt_ref.shape[1]
    chunk_pages = kbuf_ref.shape[1]
    page_size = kbuf_ref.shape[-1]
    n_chunks = n_pages // chunk_pages
    dbias = dbias_ref[...]
    hmask = hmask_ref[...]

    def chunk_copies(chunk, ring):
        first_page = n_pages - chunk_pages * (jnp.minimum(chunk, n_chunks - 1) + 1)
        copies = []
        for j in range(chunk_pages):
            page = pt_ref[seq_id, first_page + j]
            copies.append(pltpu.make_async_copy(ck_hbm.at[layer, page], kbuf_ref.at[ring, j], sem_ref.at[ring]))
            copies.append(pltpu.make_async_copy(cv_hbm.at[layer, page], vbuf_ref.at[ring, j], sem_ref.at[ring]))
        return copies

    def slot_hook(slot):
        ring = slot % 2
        for cp in chunk_copies(slot, ring):
            cp.wait()
        for cp in chunk_copies(slot + 1, 1 - ring):
            cp.start()
        yield from _decode_pages([kbuf_ref.at[ring, j] for j in range(chunk_pages)],
                                 [vbuf_ref.at[ring, j] for j in range(chunk_pages)],
                                 q_ref, hmask, dbias, dtri_ref[...], dacc_ref, dcarry_ref,
                                 valid=slot < n_chunks)

    _decode_init(q_ref, knew_ref, vnew_ref, hmask, dbias, dacc_ref, donew_ref, dcarry_ref,
                 past_len=n_pages * page_size, dec_seq=dec_seq)
    for cp in chunk_copies(0, 0):
        cp.start()
    _sweep(qi_ref, kj_ref, bias_ref, k_ref, qt_ref, vt_ref, tri_ref, o_ref,
           acc_ref, carry_ref, z_even_ref, z_odd_ref, head_dim=head_dim, unroll=unroll, slot_hook=slot_hook)
    for cp in chunk_copies(n_slots, n_slots % 2):
        cp.wait()
    od_ref[...] = _decode_output(dacc_ref, donew_ref, hmask)


def _prompt_attention(k_bf, qt_bf, vt_bf, bias, tri, *, batch, seq, head_dim, decode=None):
    m, d = k_bf.shape
    blk = ATT_BLOCK
    nq = seq // blk
    assert nq >= 2
    groups = d // LANES
    pairs = [(qi, qi) for qi in range(nq)]
    pairs += [(qi, kj) for qi in range(1, nq) for kj in range(qi - 1, -1, -1)]
    unroll = _sweep_unroll(nq)
    qi_tab = jnp.asarray([p[0] for p in pairs], jnp.int32)
    kj_tab = jnp.asarray([p[1] for p in pairs], jnp.int32)
    n_prefetch = 2 if decode is None else 3
    blocked = pl.BlockSpec((nq, LANES, blk), lambda b, g, *_: (b, g, 0))
    seq_cols = pl.BlockSpec((seq, LANES), lambda b, g, *_: (b, g))
    z_tiles = pltpu.VMEM((unroll * HEADS_PER_GROUP, blk, blk), F32)
    in_specs = [pl.BlockSpec(memory_space=pltpu.SMEM), seq_cols, blocked, blocked,
                pl.BlockSpec((blk, blk), lambda b, g, *_: (0, 0))]
    scratch = [pltpu.VMEM((nq, HEADS_PER_GROUP, head_dim, blk), F32),
               pltpu.VMEM((nq, HEADS_PER_GROUP, 1, blk), F32), z_tiles, z_tiles]
    out_sweep = jax.ShapeDtypeStruct((m, d), BF16)
    if decode is None:
        return pl.pallas_call(
            functools.partial(_attn_kernel, head_dim=head_dim, unroll=unroll),
            grid_spec=pltpu.PrefetchScalarGridSpec(
                num_scalar_prefetch=n_prefetch, grid=(batch, groups),
                in_specs=in_specs, out_specs=seq_cols, scratch_shapes=scratch),
            out_shape=out_sweep,
            compiler_params=_cparams("parallel", "parallel"),
            name="prompt_attention",
        )(qi_tab, kj_tab, bias, k_bf, qt_bf, vt_bf, tri)

    q, k_new, v_new, cache_kt, cache_vt, layer, page_table, dec_bias = decode
    db = q.shape[0]
    assert db == batch * groups
    n_heads, _, page_size = cache_kt.shape[2:]
    n_pages = page_table.shape[1]
    n_slots = _sweep_slots(nq, unroll)
    chunk_pages = _decode_chunk_pages(n_pages, n_slots)
    hmask, bias_col, dtri = _decode_operands(dec_bias, d, page_size)
    per_seq = pl.BlockSpec((None, 1, d), lambda b, g, *_: (b * groups + g, 0, 0))
    in_specs += [
        per_seq, per_seq, per_seq,
        pl.BlockSpec((n_heads, d), lambda b, g, *_: (0, 0)),
        pl.BlockSpec((n_heads, 1), lambda b, g, *_: (0, 0)),
        pl.BlockSpec((page_size, page_size), lambda b, g, *_: (0, 0)),
        pl.BlockSpec(memory_space=pl.ANY), pl.BlockSpec(memory_space=pl.ANY),
    ]
    ring = pltpu.VMEM((2, chunk_pages, n_heads, head_dim, page_size), F32)
    scratch += _decode_scratch(n_heads, d) + [ring, ring, pltpu.SemaphoreType.DMA((2,))]
    return pl.pallas_call(
        functools.partial(_attn_decode_kernel, head_dim=head_dim, unroll=unroll, layer=layer,
                          n_slots=n_slots, dec_seq=1),
        grid_spec=pltpu.PrefetchScalarGridSpec(
            num_scalar_prefetch=n_prefetch, grid=(batch, groups),
            in_specs=in_specs, out_specs=[seq_cols, per_seq], scratch_shapes=scratch),
        out_shape=[out_sweep, jax.ShapeDtypeStruct((db, 1, d), F32)],
        compiler_params=_cparams("parallel", "parallel"),
        name="prompt_attention_with_decode",
    )(qi_tab, kj_tab, page_table, bias, k_bf, qt_bf, vt_bf, tri,
      q, k_new, v_new, hmask, bias_col, dtri, cache_kt, cache_vt)


def _decode_chunk_pages(n_pages, n_slots):
    fits = [c for c in range(1, n_pages + 1) if n_pages % c == 0 and n_pages // c <= n_slots]
    return fits[0] if fits else None


def kernel(x_prompt, x_sample, cache_k, cache_v, state_conv, page_table, norm_mix_pre, norm_mix_post,
           norm_mlp_pre, norm_mlp_post, w_qkv, sb_bias, w_attn_out, w_conv_in, conv_w, w_conv_out,
           w_mlp_up, w_mlp_down):
    batch, seq, d = x_prompt.shape
    db, ts, _ = x_sample.shape
    assert ts == 1, "the sample group decodes one token per sequence"
    depth = norm_mix_pre.shape[0]
    n_heads = sb_bias.shape[1]
    head_dim = d // n_heads
    assert head_dim * HEADS_PER_GROUP == LANES
    assert seq % ATT_BLOCK == 0 and page_table.shape[1] % PAGES_PER_STEP == 0
    scale = head_dim ** -0.5
    tm = 512 if seq % 512 == 0 else ATT_BLOCK
    tm_post = 256
    nq = seq // ATT_BLOCK
    fuse_decode = (db == batch * (d // LANES) and
                   _decode_chunk_pages(page_table.shape[1], _sweep_slots(nq, _sweep_unroll(nq))) is not None)

    xp = x_prompt.reshape(batch * seq, d)
    xs = x_sample.reshape(db * ts, d)
    tri = (lax.broadcasted_iota(jnp.int32, (ATT_BLOCK, ATT_BLOCK), 1)
           >= lax.broadcasted_iota(jnp.int32, (ATT_BLOCK, ATT_BLOCK), 0)).astype(BF16)
    cache_kt = jnp.transpose(cache_k, (0, 1, 3, 4, 2))
    cache_vt = jnp.transpose(cache_v, (0, 1, 3, 4, 2))

    row = lambda g, i: g[i].reshape(1, d)
    w_qkv_bf = w_qkv.astype(BF16)
    w_qkv_t_bf = jnp.swapaxes(w_qkv, 1, 2).astype(BF16)
    w_attn_out_bf = w_attn_out.astype(BF16)
    w_conv_in_bf = w_conv_in.astype(BF16)
    w_conv_out_bf = w_conv_out.astype(BF16)
    w_mlp_up_bf = w_mlp_up.astype(BF16)
    w_mlp_down_bf = w_mlp_down.astype(BF16)
    kv_t = ()
    cp_new, ks_new, vs_new, cs_new = [], [], [], []
    for i in range(depth):
        post_w = [None, (row(norm_mix_post, i), None), (row(norm_mlp_pre, i), None), (w_mlp_up_bf, i),
                  (w_mlp_down_bf, i), (row(norm_mlp_post, i), None)]
        g_pre = row(norm_mix_pre, i)
        if i % 2 == 0:
            a = i // 2
            post_w[0] = (w_attn_out_bf, a)
            k_bf, qt_bf, kt, vt, vt_bf = _qkv_proj_prompt(xp, g_pre, w_qkv_bf, w_qkv_t_bf, a, scale * LOG2E, kv_t,
                                                          batch=batch, seq=seq, tm=tm)
            kv_t = (kt, vt)
            qs_bf, ks, vs = _qkv_proj_sample(xs, g_pre, w_qkv_bf, a, scale)
            per_seq = lambda t: t.reshape(db, 1, d)
            decode = (per_seq(qs_bf.astype(F32)), per_seq(ks), per_seq(vs), cache_kt, cache_vt,
                      a, page_table, sb_bias[a])
            if fuse_decode:
                o_bf, os_ = _prompt_attention(k_bf, qt_bf, vt_bf, sb_bias[a], tri,
                                              batch=batch, seq=seq, head_dim=head_dim, decode=decode)
            else:
                o_bf = _prompt_attention(k_bf, qt_bf, vt_bf, sb_bias[a], tri,
                                         batch=batch, seq=seq, head_dim=head_dim)
                os_ = _decode_attention(*decode)
            xp = _post_attn(o_bf, xp, post_w, tm=tm_post, name="post_attn_prompt")
            xs = _post_attn(os_.reshape(db, d).astype(BF16), xs, post_w, tm=db * ts, name="post_attn_sample")
            ks_new.append(ks.reshape(db, ts, n_heads, head_dim))
            vs_new.append(vs.reshape(db, ts, n_heads, head_dim))
        else:
            c = i // 2
            post_w[0] = (w_conv_out_bf, c)
            gb, u = _convin_proj(xp, g_pre, w_conv_in_bf, c, tm=tm, name="convin_prompt")
            xp = _post_conv_prompt(gb, u, conv_w[c], xp, post_w, tm=tm_post, seq=seq)
            cp_new.append(u.reshape(batch, seq, d)[:, seq - 2:, :])
            gbs, us = _convin_proj(xs, g_pre, w_conv_in_bf, c, tm=db * ts, name="convin_sample")
            st = state_conv[c]
            xs = _post_conv_sample(gbs, us, st[:, 0, :], st[:, 1, :], conv_w[c], xs, post_w)
            cs_new.append(jnp.concatenate([st[:, 1:, :], us.reshape(db, ts, d)], axis=1))
    heads_major = lambda t: jnp.transpose(t.reshape(-1, batch, n_heads, head_dim, seq), (0, 1, 4, 2, 3))
    return (xp.reshape(batch, seq, d), xs.reshape(db, ts, d),
            heads_major(kv_t[0]), heads_major(kv_t[1]), jnp.stack(cp_new),
            jnp.stack(ks_new), jnp.stack(vs_new), jnp.stack(cs_new))
```

```python
import functools

import jax
import jax.numpy as jnp
from jax import lax
from jax.experimental import pallas as pl
from jax.experimental.pallas import tpu as pltpu

F32 = jnp.float32
BF16 = jnp.bfloat16

RMS_EPS = 1e-6
VMEM_LIMIT_BYTES = 56 * 1024 * 1024
LANES = 128
HEADS_PER_GROUP = 2
ROW_TILE = 512
ATT_BLOCK = 256
ATT_UNROLL = 4
LOG2E = 1.4426950408889634
PAGES_PER_STEP = 8
CONV_HALO = 8


def _cparams(*sem):
    return pltpu.CompilerParams(dimension_semantics=sem, vmem_limit_bytes=VMEM_LIMIT_BYTES)


def _const_spec(shape):
    nd = len(shape)
    return pl.BlockSpec(shape, lambda *_: (0,) * nd, pipeline_mode=pl.Buffered(1))


def _rms(x, g):
    inv = lax.rsqrt(jnp.mean(x * x, axis=-1, keepdims=True) + RMS_EPS)
    return (x * inv) * g


def _softplus(z):
    return jnp.maximum(z, 0.0) + jnp.log(1.0 + jnp.exp(-jnp.abs(z)))


def _qkv_kernel(x_ref, g_ref, w_ref, q_ref, k_ref, v_ref, *, d, scale):
    xn = _rms(x_ref[...], g_ref[...]).astype(BF16)
    qkv = jnp.dot(xn, w_ref[...], preferred_element_type=F32)
    q_ref[...] = (qkv[:, :d] * scale).astype(BF16)
    k_ref[...] = qkv[:, d:2 * d]
    v_ref[...] = qkv[:, 2 * d:]


def _qkv_prompt_kernel(x_ref, g_ref, wk_ref, wt_ref, *refs, d, scale, n_prev):
    prev_refs, (kb_ref, qtb_ref, kt_ref, vt_ref, vtb_ref) = refs[:len(refs) - 5], refs[len(refs) - 5:]
    xn = _rms(x_ref[...], g_ref[...]).astype(BF16)
    kb_ref[...] = jnp.dot(xn, wk_ref[...], preferred_element_type=F32).astype(BF16)
    t = lax.dot_general(wt_ref[...], xn, (((1,), (1,)), ((), ())), preferred_element_type=F32)
    if n_prev:
        kt_prev_ref, vt_prev_ref = prev_refs
        kt_ref[:n_prev] = kt_prev_ref[...]
        vt_ref[:n_prev] = vt_prev_ref[...]
    kt_ref[n_prev] = t[d:2 * d]
    vt_ref[n_prev] = t[2 * d:]
    qtb = (t[:d] * scale).astype(BF16)
    vtb = t[2 * d:].astype(BF16)
    for c in range(vtb_ref.shape[0]):
        qtb_ref[c] = qtb[:, c * ATT_BLOCK:(c + 1) * ATT_BLOCK]
        vtb_ref[c] = vtb[:, c * ATT_BLOCK:(c + 1) * ATT_BLOCK]


def _qkv_proj_sample(x, g, w_stack, layer, scale):
    m, d = x.shape
    blk = pl.BlockSpec((m, d), lambda i: (0, 0))
    return pl.pallas_call(
        functools.partial(_qkv_kernel, d=d, scale=scale),
        grid=(1,),
        in_specs=[blk, _const_spec((1, d)), _param_spec(w_stack, layer)],
        out_specs=[blk, blk, blk],
        out_shape=[jax.ShapeDtypeStruct((m, d), BF16),
                   jax.ShapeDtypeStruct((m, d), F32),
                   jax.ShapeDtypeStruct((m, d), F32)],
        compiler_params=_cparams("arbitrary"),
        name="qkv_proj_sample",
    )(x, g, w_stack)


def _qkv_proj_prompt(x, g, w_stack, wt_stack, layer, scale, prev, *, batch, seq, tm):
    m, d = x.shape
    nb = seq // tm
    cb = tm // ATT_BLOCK
    n_prev = prev[0].shape[0] if prev else 0
    row_blk = pl.BlockSpec((tm, d), lambda b, i: (b * nb + i, 0))
    stack_blk = lambda n: pl.BlockSpec((n, None, d, tm), lambda b, i: (0, b, 0, i))
    tb_blk = pl.BlockSpec((cb, d, ATT_BLOCK), lambda b, i: (b * nb + i, 0, 0))
    tb_shape = jax.ShapeDtypeStruct((m // ATT_BLOCK, d, ATT_BLOCK), BF16)
    stack_shape = jax.ShapeDtypeStruct((n_prev + 1, batch, d, seq), F32)
    wk_spec = pl.BlockSpec((None, d, d), lambda *_: (layer, 0, 1), pipeline_mode=pl.Buffered(1))
    return pl.pallas_call(
        functools.partial(_qkv_prompt_kernel, d=d, scale=scale, n_prev=n_prev),
        grid=(batch, nb),
        in_specs=[row_blk, _const_spec((1, d)), wk_spec, _param_spec(wt_stack, layer)]
                 + [stack_blk(n_prev)] * len(prev),
        out_specs=[row_blk, tb_blk, stack_blk(n_prev + 1), stack_blk(n_prev + 1), tb_blk],
        out_shape=[jax.ShapeDtypeStruct((m, d), BF16), tb_shape, stack_shape, stack_shape, tb_shape],
        compiler_params=_cparams("parallel", "parallel"),
        name="qkv_proj_prompt",
    )(x, g, w_stack, wt_stack, *prev)


def _convin_kernel(x_ref, g_ref, w_ref, gb_ref, u_ref, *, d):
    xn = _rms(x_ref[...], g_ref[...]).astype(BF16)
    y = jnp.dot(xn, w_ref[...], preferred_element_type=F32)
    gb_ref[...] = y[:, :d]
    u_ref[...] = y[:, d:2 * d] * y[:, 2 * d:]


def _convin_proj(x, g, w_stack, layer, *, tm, name):
    m, d = x.shape
    row = lambda i: (i, 0)
    return pl.pallas_call(
        functools.partial(_convin_kernel, d=d),
        grid=(m // tm,),
        in_specs=[pl.BlockSpec((tm, d), row), _const_spec((1, d)), _param_spec(w_stack, layer)],
        out_specs=[pl.BlockSpec((tm, d), row)] * 2,
        out_shape=[jax.ShapeDtypeStruct((m, d), F32)] * 2,
        compiler_params=_cparams("parallel"),
        name=name,
    )(x, g, w_stack)


def _post_body(a_bf, x, wo_ref, gpost_ref, gpre_ref, wup_ref, wdown_ref, gmpost_ref):
    o = jnp.dot(a_bf, wo_ref[...], preferred_element_type=F32)
    x1 = x + _rms(o, gpost_ref[...])
    hn = _rms(x1, gpre_ref[...]).astype(BF16)
    h = jnp.dot(hn, wup_ref[...], preferred_element_type=F32)
    h = jnp.square(jnp.maximum(h, 0.0)).astype(BF16)
    mlp = jnp.dot(h, wdown_ref[...], preferred_element_type=F32)
    return x1 + _rms(mlp, gmpost_ref[...])


def _post_attn_kernel(a_ref, x_ref, wo_ref, gpost_ref, gpre_ref, wup_ref, wdown_ref, gmpost_ref, o_ref):
    o_ref[...] = _post_body(a_ref[...], x_ref[...], wo_ref, gpost_ref, gpre_ref,
                            wup_ref, wdown_ref, gmpost_ref)


def _post_conv_prompt_kernel(gb_ref, u_ref, halo_ref, cw_ref, x_ref, wo_ref, gpost_ref, gpre_ref,
                             wup_ref, wdown_ref, gmpost_ref, o_ref, *, blocks_per_seq):
    u = u_ref[...]
    seq_start = (pl.program_id(0) % blocks_per_seq) == 0
    halo = jnp.where(seq_start, 0.0, halo_ref[...])
    r = lax.broadcasted_iota(jnp.int32, u.shape, 0)
    u1 = jnp.where(r == 0, halo[CONV_HALO - 1:CONV_HALO, :], pltpu.roll(u, 1, axis=0))
    u2 = pltpu.roll(u, 2, axis=0)
    u2 = jnp.where(r == 0, halo[CONV_HALO - 2:CONV_HALO - 1, :], u2)
    u2 = jnp.where(r == 1, halo[CONV_HALO - 1:CONV_HALO, :], u2)
    cw = cw_ref[...]
    y = cw[0:1, :] * u2 + cw[1:2, :] * u1 + cw[2:3, :] * u
    a = (gb_ref[...] * y).astype(BF16)
    o_ref[...] = _post_body(a, x_ref[...], wo_ref, gpost_ref, gpre_ref, wup_ref, wdown_ref, gmpost_ref)


def _post_conv_sample_kernel(gb_ref, u_ref, s0_ref, s1_ref, cw_ref, x_ref, wo_ref, gpost_ref, gpre_ref,
                             wup_ref, wdown_ref, gmpost_ref, o_ref):
    cw = cw_ref[...]
    y = cw[0:1, :] * s0_ref[...] + cw[1:2, :] * s1_ref[...] + cw[2:3, :] * u_ref[...]
    a = (gb_ref[...] * y).astype(BF16)
    o_ref[...] = _post_body(a, x_ref[...], wo_ref, gpost_ref, gpre_ref, wup_ref, wdown_ref, gmpost_ref)


def _param_spec(arr, layer):
    if layer is None:
        return _const_spec(arr.shape)
    tail = (0,) * (arr.ndim - 1)
    return pl.BlockSpec((None,) + arr.shape[1:], lambda *_: (layer,) + tail, pipeline_mode=pl.Buffered(1))


def _post_weight_specs(weights):
    return [_param_spec(arr, layer) for arr, layer in weights]


def _post_attn(a, x, weights, *, tm, name):
    m, d = x.shape
    row = lambda i: (i, 0)
    blk = pl.BlockSpec((tm, d), row)
    return pl.pallas_call(
        _post_attn_kernel,
        grid=(m // tm,),
        in_specs=[blk, blk] + _post_weight_specs(weights),
        out_specs=blk,
        out_shape=jax.ShapeDtypeStruct((m, d), F32),
        compiler_params=_cparams("parallel"),
        name=name,
    )(a, x, *[arr for arr, _ in weights])


def _post_conv_prompt(gb, u, cw, x, weights, *, tm, seq):
    m, d = x.shape
    row = lambda i: (i, 0)
    blk = pl.BlockSpec((tm, d), row)
    halo_blocks = tm // CONV_HALO
    halo = pl.BlockSpec((CONV_HALO, d), lambda i: (jnp.maximum(i * halo_blocks - 1, 0), 0))
    return pl.pallas_call(
        functools.partial(_post_conv_prompt_kernel, blocks_per_seq=seq // tm),
        grid=(m // tm,),
        in_specs=[blk, blk, halo, _const_spec((3, d)), blk] + _post_weight_specs(weights),
        out_specs=blk,
        out_shape=jax.ShapeDtypeStruct((m, d), F32),
        compiler_params=_cparams("parallel"),
        name="post_conv_prompt",
    )(gb, u, u, cw, x, *[arr for arr, _ in weights])


def _post_conv_sample(gb, u, s0, s1, cw, x, weights):
    m, d = x.shape
    blk = pl.BlockSpec((m, d), lambda i: (0, 0))
    return pl.pallas_call(
        _post_conv_sample_kernel,
        grid=(1,),
        in_specs=[blk, blk, blk, blk, _const_spec((3, d)), blk] + _post_weight_specs(weights),
        out_specs=blk,
        out_shape=jax.ShapeDtypeStruct((m, d), F32),
        compiler_params=_cparams("arbitrary"),
        name="post_conv_sample",
    )(gb, u, s0, s1, cw, x, *[arr for arr, _ in weights])


def _softplus2(z2):
    return jnp.maximum(z2, 0.0) + jnp.log(1.0 + jnp.exp2(-jnp.abs(z2))) * LOG2E


def _sweep_unroll(nq):
    n_visits = nq * (nq + 1) // 2
    return max(u for u in range(1, ATT_UNROLL + 1) if nq % u == 0 and n_visits % u == 0)


def _sweep_trips(nq, unroll):
    diag_trips = nq // unroll
    n_trips = (nq * (nq + 1) // 2) // unroll
    return [(0, diag_trips, True), (diag_trips, n_trips - diag_trips, False)]


def _sweep_slots(nq, unroll):
    slots = 0
    for first, count, _ in _sweep_trips(nq, unroll):
        head = 1 if count and first % 2 else 0
        slots += head + (count - head) // 2 + (count - head) % 2
    return slots


def _sweep(qi_ref, kj_ref, bias_ref, k_ref, qt_ref, vt_ref, tri_ref, o_ref,
           acc_ref, carry_ref, z_even_ref, z_odd_ref, *, head_dim, unroll, slot_hook):
    hp = pl.program_id(1)
    blk = ATT_BLOCK
    nq = acc_ref.shape[0]
    n_trips = (nq * (nq + 1) // 2) // unroll
    biases = [bias_ref[hp * HEADS_PER_GROUP + h] * LOG2E for h in range(HEADS_PER_GROUP)]
    z_refs = (z_even_ref, z_odd_ref)

    def visits_of(trip):
        return [(qi_ref[trip * unroll + u], kj_ref[trip * unroll + u], h)
                for u in range(unroll) for h in range(HEADS_PER_GROUP)]

    def scores(trip, z_ref):
        for c, (qi, kj, h) in enumerate(visits_of(trip)):
            qt = qt_ref[qi]
            feat = lax.broadcasted_iota(jnp.int32, qt.shape, 0)
            own = (feat < head_dim) if h == 0 else (feat >= head_dim)
            qt_h = jnp.where(own, qt, jnp.zeros_like(qt))
            kb = k_ref[pl.ds(pl.multiple_of(kj * blk, blk), blk), :]
            z_ref[c] = jnp.dot(kb, qt_h, preferred_element_type=F32) + biases[h]

    def finish(trip, z_ref, diagonal):
        visits = visits_of(trip)
        if diagonal:
            key_idx = lax.broadcasted_iota(jnp.int32, (blk, blk), 0)
            qry_idx = lax.broadcasted_iota(jnp.int32, (blk, blk), 1)
            causal = key_idx < qry_idx
        sufs = []
        for c in range(len(visits)):
            sp = _softplus2(z_ref[c])
            if diagonal:
                sp = jnp.where(causal, sp, 0.0)
            sufs.append(jnp.dot(tri_ref[...], sp.astype(BF16), preferred_element_type=F32))
        yield
        outs = []
        for c, (qi, kj, h) in enumerate(visits):
            w = jnp.exp2(z_ref[c] - sufs[c])
            if diagonal:
                w = jnp.where(causal, w, 0.0)
            vt = vt_ref[kj, h * head_dim:(h + 1) * head_dim, :]
            outs.append(jnp.dot(vt, w.astype(BF16), preferred_element_type=F32))
        yield
        for (qi, kj, h), ob, suf in zip(visits, outs, sufs):
            tot = suf[0:1, :]
            if diagonal:
                acc_ref[qi, h] = ob
                carry_ref[qi, h] = tot
            else:
                carry = carry_ref[qi, h]
                acc_ref[qi, h] = acc_ref[qi, h] + ob * jnp.exp2(-carry)
                carry_ref[qi, h] = carry + tot

    def loop_body(slot, trips, diagonal):
        rider = slot_hook(slot)
        next(rider, None)
        for n, (trip, parity) in enumerate(trips):
            scores(jnp.minimum(trip + 1, n_trips - 1), z_refs[1 - parity])
            stages = finish(trip, z_refs[parity], diagonal)
            next(stages)
            if n == 0:
                next(rider, None)
            next(stages)
            if n == 0:
                next(rider, None)
            next(stages, None)
        next(rider, None)

    def run(first, count, diagonal, slot):
        if count and first % 2:
            loop_body(slot, [(first, 1)], diagonal)
            first, count, slot = first + 1, count - 1, slot + 1

        def pair(p, c):
            loop_body(slot + p, [(first + 2 * p, 0), (first + 2 * p + 1, 1)], diagonal)
            return c

        lax.fori_loop(0, count // 2, pair, 0)
        slot += count // 2
        if count % 2:
            loop_body(slot, [(first + count - 1, 0)], diagonal)
            slot += 1
        return slot

    scores(0, z_even_ref)
    slot = 0
    for first, count, diagonal in _sweep_trips(nq, unroll):
        slot = run(first, count, diagonal, slot)

    def out_body(qi, c):
        ot = jnp.concatenate([acc_ref[qi, h] for h in range(HEADS_PER_GROUP)], axis=0)
        o_ref[pl.ds(pl.multiple_of(qi * blk, blk), blk), :] = ot.T.astype(BF16)
        return c

    lax.fori_loop(0, nq, out_body, 0)


def _attn_kernel(qi_ref, kj_ref, *refs, head_dim, unroll):
    _sweep(qi_ref, kj_ref, *refs, head_dim=head_dim, unroll=unroll, slot_hook=lambda slot: iter(()))


def _decode_init(q_ref, knew_ref, vnew_ref, hmask, bias, acc_ref, onew_ref, carry_ref, *, past_len, dec_seq):
    n_heads = hmask.shape[0]
    q_pos = past_len + dec_seq - 1
    k_pos = past_len + lax.broadcasted_iota(jnp.int32, (n_heads, dec_seq), 1)
    visible = k_pos < q_pos
    z = jnp.sum(q_ref[...] * knew_ref[...] * hmask, axis=1, keepdims=True) + bias
    sp = jnp.where(visible, _softplus(z), 0.0)
    w = jnp.where(visible, jnp.exp(z - sp), 0.0)
    onew_ref[...] = jnp.sum(w * hmask, axis=0, keepdims=True) * vnew_ref[...]
    carry_ref[...] = sp
    acc_ref[...] = jnp.zeros_like(acc_ref)


def _decode_pages(k_pages, v_pages, q_ref, hmask, bias, tri, acc_ref, carry_ref, valid=None):
    n_heads = hmask.shape[0]
    n = len(k_pages)
    page = tri.shape[0]
    flat = lambda ref: ref[...].reshape(-1, page).astype(BF16)
    q_bd = (q_ref[...] * hmask).astype(BF16)
    kcat = jnp.concatenate([flat(kp) for kp in k_pages], axis=1)
    z = jnp.dot(q_bd, kcat, preferred_element_type=F32) + bias
    yield
    zs = [z[:, p * page:(p + 1) * page] for p in range(n)]
    sp = jnp.concatenate([_softplus(zp) for zp in zs], axis=0)
    if valid is not None:
        sp = jnp.where(valid, sp, 0.0)
    sp_hi = sp.astype(BF16)
    sp_lo = (sp - sp_hi.astype(F32)).astype(BF16)
    suf = (jnp.dot(sp_hi, tri, preferred_element_type=F32)
           + jnp.dot(sp_lo, tri, preferred_element_type=F32))
    yield
    carry = carry_ref[...]
    ws = [None] * n
    for p in reversed(range(n)):
        suf_p = suf[p * n_heads:(p + 1) * n_heads, :]
        w = jnp.exp(zs[p] - suf_p - carry)
        if valid is not None:
            w = jnp.where(valid, w, 0.0)
        ws[p] = w.astype(BF16)
        carry = carry + suf_p[:, 0:1]
    w_all = jnp.concatenate(ws, axis=1)
    vcat = jnp.concatenate([flat(vp) for vp in v_pages], axis=1)
    pv = lax.dot_general(w_all, vcat, (((1,), (1,)), ((), ())),
                         preferred_element_type=F32)
    yield
    acc_ref[...] += pv
    carry_ref[...] = carry


def _decode_output(acc_ref, onew_ref, hmask):
    return jnp.sum(acc_ref[...] * hmask, axis=0, keepdims=True) + onew_ref[...]


def _decode_kernel(pt_ref, q_ref, knew_ref, vnew_ref, hmask_ref, bias_ref, tri_ref, *refs,
                   n_pages, page_size, dec_seq):
    del pt_ref
    pps = PAGES_PER_STEP
    k_refs = refs[:pps]
    v_refs = refs[pps:2 * pps]
    o_ref = refs[2 * pps]
    acc_ref, onew_ref, carry_ref = refs[2 * pps + 1:]
    step = pl.program_id(1)
    bias = bias_ref[...]
    hmask = hmask_ref[...]

    @pl.when(step == 0)
    def _():
        _decode_init(q_ref, knew_ref, vnew_ref, hmask, bias, acc_ref, onew_ref, carry_ref,
                     past_len=n_pages * page_size, dec_seq=dec_seq)

    for _ in _decode_pages(k_refs, v_refs, q_ref, hmask, bias, tri_ref[...], acc_ref, carry_ref):
        pass

    @pl.when(step == pl.num_programs(1) - 1)
    def _():
        o_ref[...] = _decode_output(acc_ref, onew_ref, hmask)


def _decode_operands(bias, d, page_size):
    n_heads = bias.shape[0]
    head_dim = d // n_heads
    hmask = (lax.broadcasted_iota(jnp.int32, (n_heads, d), 1) // head_dim
             == lax.broadcasted_iota(jnp.int32, (n_heads, d), 0)).astype(F32)
    tri = (lax.broadcasted_iota(jnp.int32, (page_size, page_size), 0)
           >= lax.broadcasted_iota(jnp.int32, (page_size, page_size), 1)).astype(BF16)
    return hmask, bias.reshape(n_heads, 1), tri


def _decode_scratch(n_heads, d):
    return [pltpu.VMEM((n_heads, d), F32), pltpu.VMEM((1, d), F32), pltpu.VMEM((n_heads, 1), F32)]


def _decode_attention(q, k_new, v_new, cache_kt, cache_vt, layer, page_table, bias):
    db, _, d = q.shape
    n_heads, head_dim, page_size = cache_kt.shape[2:]
    n_pages = page_table.shape[1]
    pps = PAGES_PER_STEP
    steps = n_pages // pps
    hmask, bias_col, tri = _decode_operands(bias, d, page_size)

    def page_spec(p):
        return pl.BlockSpec((None, None, n_heads, head_dim, page_size),
                            lambda b, s, pt: (layer, pt[b, (steps - 1 - s) * pps + p], 0, 0, 0))

    per_seq = pl.BlockSpec((None, 1, d), lambda b, s, pt: (b, 0, 0))
    grid_spec = pltpu.PrefetchScalarGridSpec(
        num_scalar_prefetch=1,
        grid=(db, steps),
        in_specs=[
            per_seq, per_seq, per_seq,
            pl.BlockSpec((n_heads, d), lambda b, s, pt: (0, 0)),
            pl.BlockSpec((n_heads, 1), lambda b, s, pt: (0, 0)),
            pl.BlockSpec((page_size, page_size), lambda b, s, pt: (0, 0)),
        ] + [page_spec(p) for p in range(pps)] * 2,
        out_specs=per_seq,
        scratch_shapes=_decode_scratch(n_heads, d),
    )
    return pl.pallas_call(
        functools.partial(_decode_kernel, n_pages=n_pages, page_size=page_size, dec_seq=1),
        grid_spec=grid_spec,
        out_shape=jax.ShapeDtypeStruct((db, 1, d), F32),
        compiler_params=_cparams("parallel", "arbitrary"),
        name="decode_attention",
    )(page_table, q, k_new, v_new, hmask, bias_col, tri,
      *([cache_kt] * pps), *([cache_vt] * pps))


def _attn_decode_kernel(qi_ref, kj_ref, pt_ref, bias_ref, k_ref, qt_ref, vt_ref, tri_ref,
                        q_ref, knew_ref, vnew_ref, hmask_ref, dbias_ref, dtri_ref, ck_hbm, cv_hbm,
                        o_ref, od_ref,
                        acc_ref, carry_ref, z_even_ref, z_odd_ref,
                        dacc_ref, donew_ref, dcarry_ref, kbuf_ref, vbuf_ref, sem_ref,
                        *, head_dim, unroll, layer, n_slots, dec_seq):
    seq_id = pl.program_id(0) * pl.num_programs(1) + pl.program_id(1)
    n_pages = pt_ref.shape[1]
    chunk_pages = kbuf_ref.shape[1]
    page_size = kbuf_ref.shape[-1]
    n_chunks = n_pages // chunk_pages
    dbias = dbias_ref[...]
    hmask = hmask_ref[...]

    def chunk_copies(chunk, ring):
        first_page = n_pages - chunk_pages * (jnp.minimum(chunk, n_chunks - 1) + 1)
        copies = []
        for j in range(chunk_pages):
            page = pt_ref[seq_id, first_page + j]
            copies.append(pltpu.make_async_copy(ck_hbm.at[layer, page], kbuf_ref.at[ring, j], sem_ref.at[ring]))
            copies.append(pltpu.make_async_copy(cv_hbm.at[layer, page], vbuf_ref.at[ring, j], sem_ref.at[ring]))
        return copies

    def slot_hook(slot):
        ring = slot % 2
        for cp in chunk_copies(slot, ring):
            cp.wait()
        for cp in chunk_copies(slot + 1, 1 - ring):
            cp.start()
        yield from _decode_pages([kbuf_ref.at[ring, j] for j in range(chunk_pages)],
                                 [vbuf_ref.at[ring, j] for j in range(chunk_pages)],
                                 q_ref, hmask, dbias, dtri_ref[...], dacc_ref, dcarry_ref,
                                 valid=slot < n_chunks)

    _decode_init(q_ref, knew_ref, vnew_ref, hmask, dbias, dacc_ref, donew_ref, dcarry_ref,
                 past_len=n_pages * page_size, dec_seq=dec_seq)
    for cp in chunk_copies(0, 0):
        cp.start()
    _sweep(qi_ref, kj_ref, bias_ref, k_ref, qt_ref, vt_ref, tri_ref, o_ref,
           acc_ref, carry_ref, z_even_ref, z_odd_ref, head_dim=head_dim, unroll=unroll, slot_hook=slot_hook)
    for cp in chunk_copies(n_slots, n_slots % 2):
        cp.wait()
    od_ref[...] = _decode_output(dacc_ref, donew_ref, hmask)


def _prompt_attention(k_bf, qt_bf, vt_bf, bias, tri, *, batch, seq, head_dim, decode=None):
    m, d = k_bf.shape
    blk = ATT_BLOCK
    nq = seq // blk
    assert nq >= 2
    groups = d // LANES
    pairs = [(qi, qi) for qi in range(nq)]
    pairs += [(qi, kj) for qi in range(1, nq) for kj in range(qi - 1, -1, -1)]
    unroll = _sweep_unroll(nq)
    qi_tab = jnp.asarray([p[0] for p in pairs], jnp.int32)
    kj_tab = jnp.asarray([p[1] for p in pairs], jnp.int32)
    n_prefetch = 2 if decode is None else 3
    blocked = pl.BlockSpec((nq, LANES, blk), lambda b, g, *_: (b, g, 0))
    seq_cols = pl.BlockSpec((seq, LANES), lambda b, g, *_: (b, g))
    z_tiles = pltpu.VMEM((unroll * HEADS_PER_GROUP, blk, blk), F32)
    in_specs = [pl.BlockSpec(memory_space=pltpu.SMEM), seq_cols, blocked, blocked,
                pl.BlockSpec((blk, blk), lambda b, g, *_: (0, 0))]
    scratch = [pltpu.VMEM((nq, HEADS_PER_GROUP, head_dim, blk), F32),
               pltpu.VMEM((nq, HEADS_PER_GROUP, 1, blk), F32), z_tiles, z_tiles]
    out_sweep = jax.ShapeDtypeStruct((m, d), BF16)
    if decode is None:
        return pl.pallas_call(
            functools.partial(_attn_kernel, head_dim=head_dim, unroll=unroll),
            grid_spec=pltpu.PrefetchScalarGridSpec(
                num_scalar_prefetch=n_prefetch, grid=(batch, groups),
                in_specs=in_specs, out_specs=seq_cols, scratch_shapes=scratch),
            out_shape=out_sweep,
            compiler_params=_cparams("parallel", "parallel"),
            name="prompt_attention",
        )(qi_tab, kj_tab, bias, k_bf, qt_bf, vt_bf, tri)

    q, k_new, v_new, cache_kt, cache_vt, layer, page_table, dec_bias = decode
    db = q.shape[0]
    assert db == batch * groups
    n_heads, _, page_size = cache_kt.shape[2:]
    n_pages = page_table.shape[1]
    n_slots = _sweep_slots(nq, unroll)
    chunk_pages = _decode_chunk_pages(n_pages, n_slots)
    hmask, bias_col, dtri = _decode_operands(dec_bias, d, page_size)
    per_seq = pl.BlockSpec((None, 1, d), lambda b, g, *_: (b * groups + g, 0, 0))
    in_specs += [
        per_seq, per_seq, per_seq,
        pl.BlockSpec((n_heads, d), lambda b, g, *_: (0, 0)),
        pl.BlockSpec((n_heads, 1), lambda b, g, *_: (0, 0)),
        pl.BlockSpec((page_size, page_size), lambda b, g, *_: (0, 0)),
        pl.BlockSpec(memory_space=pl.ANY), pl.BlockSpec(memory_space=pl.ANY),
    ]
    ring = pltpu.VMEM((2, chunk_pages, n_heads, head_dim, page_size), F32)
    scratch += _decode_scratch(n_heads, d) + [ring, ring, pltpu.SemaphoreType.DMA((2,))]
    return pl.pallas_call(
        functools.partial(_attn_decode_kernel, head_dim=head_dim, unroll=unroll, layer=layer,
                          n_slots=n_slots, dec_seq=1),
        grid_spec=pltpu.PrefetchScalarGridSpec(
            num_scalar_prefetch=n_prefetch, grid=(batch, groups),
            in_specs=in_specs, out_specs=[seq_cols, per_seq], scratch_shapes=scratch),
        out_shape=[out_sweep, jax.ShapeDtypeStruct((db, 1, d), F32)],
        compiler_params=_cparams("parallel", "parallel"),
        name="prompt_attention_with_decode",
    )(qi_tab, kj_tab, page_table, bias, k_bf, qt_bf, vt_bf, tri,
      q, k_new, v_new, hmask, bias_col, dtri, cache_kt, cache_vt)


def _decode_chunk_pages(n_pages, n_slots):
    fits = [c for c in range(1, n_pages + 1) if n_pages % c == 0 and n_pages // c <= n_slots]
    return fits[0] if fits else None


def kernel(x_prompt, x_sample, cache_k, cache_v, state_conv, page_table, norm_mix_pre, norm_mix_post,
           norm_mlp_pre, norm_mlp_post, w_qkv, sb_bias, w_attn_out, w_conv_in, conv_w, w_conv_out,
           w_mlp_up, w_mlp_down):
    batch, seq, d = x_prompt.shape
    db, ts, _ = x_sample.shape
    assert ts == 1, "the sample group decodes one token per sequence"
    depth = norm_mix_pre.shape[0]
    n_heads = sb_bias.shape[1]
    head_dim = d // n_heads
    assert head_dim * HEADS_PER_GROUP == LANES
    assert seq % ATT_BLOCK == 0 and page_table.shape[1] % PAGES_PER_STEP == 0
    scale = head_dim ** -0.5
    tm = ROW_TILE if seq % ROW_TILE == 0 else ATT_BLOCK
    nq = seq // ATT_BLOCK
    fuse_decode = (db == batch * (d // LANES) and
                   _decode_chunk_pages(page_table.shape[1], _sweep_slots(nq, _sweep_unroll(nq))) is not None)

    xp = x_prompt.reshape(batch * seq, d)
    xs = x_sample.reshape(db * ts, d)
    tri = (lax.broadcasted_iota(jnp.int32, (ATT_BLOCK, ATT_BLOCK), 1)
           >= lax.broadcasted_iota(jnp.int32, (ATT_BLOCK, ATT_BLOCK), 0)).astype(BF16)
    cache_kt = jnp.transpose(cache_k, (0, 1, 3, 4, 2))
    cache_vt = jnp.transpose(cache_v, (0, 1, 3, 4, 2))

    row = lambda g, i: g[i].reshape(1, d)
    w_qkv_bf = w_qkv.astype(BF16)
    w_qkv_t_bf = jnp.swapaxes(w_qkv, 1, 2).astype(BF16)
    w_attn_out_bf = w_attn_out.astype(BF16)
    w_conv_in_bf = w_conv_in.astype(BF16)
    w_conv_out_bf = w_conv_out.astype(BF16)
    w_mlp_up_bf = w_mlp_up.astype(BF16)
    w_mlp_down_bf = w_mlp_down.astype(BF16)
    kv_t = ()
    cp_new, ks_new, vs_new, cs_new = [], [], [], []
    for i in range(depth):
        post_w = [None, (row(norm_mix_post, i), None), (row(norm_mlp_pre, i), None), (w_mlp_up_bf, i),
                  (w_mlp_down_bf, i), (row(norm_mlp_post, i), None)]
        g_pre = row(norm_mix_pre, i)
        if i % 2 == 0:
            a = i // 2
            post_w[0] = (w_attn_out_bf, a)
            k_bf, qt_bf, kt, vt, vt_bf = _qkv_proj_prompt(xp, g_pre, w_qkv_bf, w_qkv_t_bf, a, scale * LOG2E, kv_t,
                                                          batch=batch, seq=seq, tm=tm)
            kv_t = (kt, vt)
            qs_bf, ks, vs = _qkv_proj_sample(xs, g_pre, w_qkv_bf, a, scale)
            per_seq = lambda t: t.reshape(db, 1, d)
            decode = (per_seq(qs_bf.astype(F32)), per_seq(ks), per_seq(vs), cache_kt, cache_vt,
                      a, page_table, sb_bias[a])
            if fuse_decode:
                o_bf, os_ = _prompt_attention(k_bf, qt_bf, vt_bf, sb_bias[a], tri,
                                              batch=batch, seq=seq, head_dim=head_dim, decode=decode)
            else:
                o_bf = _prompt_attention(k_bf, qt_bf, vt_bf, sb_bias[a], tri,
                                         batch=batch, seq=seq, head_dim=head_dim)
                os_ = _decode_attention(*decode)
            xp = _post_attn(o_bf, xp, post_w, tm=tm, name="post_attn_prompt")
            xs = _post_attn(os_.reshape(db, d).astype(BF16), xs, post_w, tm=db * ts, name="post_attn_sample")
            ks_new.append(ks.reshape(db, ts, n_heads, head_dim))
            vs_new.append(vs.reshape(db, ts, n_heads, head_dim))
        else:
            c = i // 2
            post_w[0] = (w_conv_out_bf, c)
            gb, u = _convin_proj(xp, g_pre, w_conv_in_bf, c, tm=tm, name="convin_prompt")
            xp = _post_conv_prompt(gb, u, conv_w[c], xp, post_w, tm=tm, seq=seq)
            cp_new.append(u.reshape(batch, seq, d)[:, seq - 2:, :])
            gbs, us = _convin_proj(xs, g_pre, w_conv_in_bf, c, tm=db * ts, name="convin_sample")
            st = state_conv[c]
            xs = _post_conv_sample(gbs, us, st[:, 0, :], st[:, 1, :], conv_w[c], xs, post_w)
            cs_new.append(jnp.concatenate([st[:, 1:, :], us.reshape(db, ts, d)], axis=1))
    heads_major = lambda t: jnp.transpose(t.reshape(-1, batch, n_heads, head_dim, seq), (0, 1, 4, 2, 3))
    return (xp.reshape(batch, seq, d), xs.reshape(db, ts, d),
            heads_major(kv_t[0]), heads_major(kv_t[1]), jnp.stack(cp_new),
            jnp.stack(ks_new), jnp.stack(vs_new), jnp.stack(cs_new))
```
